```python
import math
import jax, jax.numpy as jnp
from jax import lax
import numpy as np

D_MODEL = 2048
BATCH = 1
SEQ = 16384
DEPTH = 2

GRID_W = 64
CTX_LEN = 256
N_MOD = 9
D_FF = ((8 * D_MODEL // 3 + 255) // 256) * 256
EPS = 1e-6

HEAD_DIM = 128
ATTN_HEADS = 8
ATTN_KV_HEADS = 2
ATTN_GROUPS = ATTN_HEADS // ATTN_KV_HEADS
ATTN_WIDTH = ATTN_HEADS * HEAD_DIM
KV_WIDTH = ATTN_KV_HEADS * HEAD_DIM
Q_BLOCK = 128
ROPE_THETA = 10000.0
ROPE_AXIS_DIM = HEAD_DIM // 2

SSD_HEADS = 8
SSD_HEAD_DIM = 64
SSD_WIDTH = SSD_HEADS * SSD_HEAD_DIM
SSD_GROUPS = 2
SSD_STATE = 128
SSD_CONV = 7
SSD_CHUNK = 128
SSD_CONV_CH = SSD_WIDTH + 2 * SSD_GROUPS * SSD_STATE
DT_MIN = 0.001
DT_MAX = 0.1

CM_WIDTH = 512
CM_KERNEL = 31

D_MIX = ATTN_WIDTH + SSD_WIDTH + CM_WIDTH
IN_COLS = ATTN_WIDTH + 2 * KV_WIDTH + SSD_WIDTH + SSD_CONV_CH + 2 * SSD_HEADS + 2 * CM_WIDTH

kernel_name = 'hybrid_dit_gqa_ssd_conformer_block'


def rms_norm(x, g):
    xf = x.astype(jnp.float32)
    y = xf * lax.rsqrt(jnp.mean(xf * xf, axis=-1, keepdims=True) + EPS)
    return (y * g.astype(jnp.float32)).astype(x.dtype)


def layer_norm(x, g, b):
    xf = x.astype(jnp.float32)
    mu = jnp.mean(xf, axis=-1, keepdims=True)
    var = jnp.mean(jnp.square(xf - mu), axis=-1, keepdims=True)
    y = (xf - mu) * lax.rsqrt(var + EPS) * g.astype(jnp.float32) + b.astype(jnp.float32)
    return y.astype(x.dtype)


def modulate(h, shift, scale):
    return h * (1.0 + scale) + shift


def swiglu(h, w_in, w_out):
    gt, up = jnp.split(h @ w_in, 2, axis=-1)
    return (jax.nn.silu(gt) * up) @ w_out


def dwconv_centred(x, w, b):
    k = w.shape[0]
    pad = k // 2
    y = lax.conv_general_dilated(x, w[:, None, :].astype(x.dtype), window_strides=(1,),
                                 padding=[(pad, pad)], dimension_numbers=('NWC', 'WIO', 'NWC'),
                                 feature_group_count=x.shape[-1])
    return y + b.astype(x.dtype)


def split_cols(p):
    o1 = ATTN_WIDTH
    o2 = o1 + KV_WIDTH
    o3 = o2 + KV_WIDTH
    o4 = o3 + SSD_WIDTH
    o5 = o4 + SSD_CONV_CH
    o6 = o5 + 2 * SSD_HEADS
    return jnp.split(p, [o1, o2, o3, o4, o5, o6], axis=-1)


def to_heads(x, n):
    return x.reshape(x.shape[0], x.shape[1], n, HEAD_DIM)


def rope_tables(n_tokens):
    rows = n_tokens // GRID_W
    row = jnp.repeat(jnp.arange(rows), GRID_W).astype(jnp.float32)
    col = jnp.tile(jnp.arange(GRID_W), rows).astype(jnp.float32)
    inv = ROPE_THETA ** (-jnp.arange(0, ROPE_AXIS_DIM, 2, dtype=jnp.float32) / ROPE_AXIS_DIM)
    ang = jnp.stack([row[:, None] * inv, col[:, None] * inv], axis=1)
    return jnp.cos(ang), jnp.sin(ang)


def apply_rope(x, cos, sin):
    bsz, n, h, d = x.shape
    xr = x.astype(jnp.float32).reshape(bsz, n, h, 2, 2, ROPE_AXIS_DIM // 2)
    x1 = xr[..., 0, :]
    x2 = xr[..., 1, :]
    cs = cos[None, :, None]
    sn = sin[None, :, None]
    out = jnp.stack([x1 * cs - x2 * sn, x2 * cs + x1 * sn], axis=-2)
    return out.reshape(bsz, n, h, d).astype(x.dtype)


def attend(q, k, v):
    bsz, lq = q.shape[:2]
    nb = lq // Q_BLOCK
    qb = q.reshape(bsz, nb, Q_BLOCK, ATTN_KV_HEADS, ATTN_GROUPS, HEAD_DIM).transpose(1, 0, 2, 3, 4, 5)
    scale = HEAD_DIM ** -0.5

    def block(qi):
        s = jnp.einsum('bqhgd,bkhd->bhgqk', qi, k).astype(jnp.float32) * scale
        pr = jax.nn.softmax(s, axis=-1).astype(v.dtype)
        return jnp.einsum('bhgqk,bkhd->bqhgd', pr, v)

    o = lax.map(block, qb)
    return o.transpose(1, 0, 2, 3, 4, 5).reshape(bsz, lq, ATTN_WIDTH)


def ssd_scan(xs, dt, a, bs, cs, h0, need_y):
    bsz, n, h, pdim = xs.shape
    nst = bs.shape[-1]
    nc = n // SSD_CHUNK
    f32 = jnp.float32
    x_c = xs.astype(f32).reshape(bsz, nc, SSD_CHUNK, h, pdim)
    b_c = bs.astype(f32).reshape(bsz, nc, SSD_CHUNK, h, nst)
    c_c = cs.astype(f32).reshape(bsz, nc, SSD_CHUNK, h, nst)
    dt_c = dt.reshape(bsz, nc, SSD_CHUNK, h)
    acum = jnp.cumsum(dt_c * a, axis=2)
    a_end = acum[:, :, -1:, :]
    states = jnp.einsum('bcjh,bcjhn,bcjhp->bchpn', jnp.exp(a_end - acum) * dt_c, b_c, x_c)
    chunk_decay = jnp.exp(a_end[:, :, 0, :])

    def step(hs, inp):
        s, d = inp
        return d[:, :, None, None] * hs + s, hs

    h_final, h_prev = lax.scan(step, h0, (jnp.moveaxis(states, 1, 0), jnp.moveaxis(chunk_decay, 1, 0)))
    if not need_y:
        return None, h_final
    h_prev = jnp.moveaxis(h_prev, 0, 1)
    lower = jnp.tril(jnp.ones((SSD_CHUNK, SSD_CHUNK), dtype=bool))
    seg = acum[:, :, :, None, :] - acum[:, :, None, :, :]
    decay = jnp.exp(jnp.where(lower[None, None, :, :, None], seg, -jnp.inf))
    scores = jnp.einsum('bcihn,bcjhn->bcijh', c_c, b_c) * decay * dt_c[:, :, None, :, :]
    y_diag = jnp.einsum('bcijh,bcjhp->bcihp', scores, x_c)
    y_off = jnp.einsum('bcihn,bchpn->bcihp', c_c, h_prev) * jnp.exp(acum)[..., None]
    return (y_diag + y_off).reshape(bsz, n, h, pdim), h_final


def ssd_side(xbc, dt_raw, conv_w, conv_b, dt_bias, a_log, h0_f, h0_b, need_y):
    bsz, n = xbc.shape[:2]
    xbc = jax.nn.silu(dwconv_centred(xbc, conv_w, conv_b))
    xs, bs, cs = jnp.split(xbc, [SSD_WIDTH, SSD_WIDTH + SSD_GROUPS * SSD_STATE], axis=-1)
    xs = xs.reshape(bsz, n, SSD_HEADS, SSD_HEAD_DIM)
    rep = SSD_HEADS // SSD_GROUPS
    bs = jnp.repeat(bs.reshape(bsz, n, SSD_GROUPS, SSD_STATE), rep, axis=2)
    cs = jnp.repeat(cs.reshape(bsz, n, SSD_GROUPS, SSD_STATE), rep, axis=2)
    dt = jax.nn.softplus(dt_raw.reshape(bsz, n, 2, SSD_HEADS).astype(jnp.float32) + dt_bias.astype(jnp.float32))
    a = -jnp.exp(a_log.astype(jnp.float32))
    y_f, h_f = ssd_scan(xs, dt[:, :, 0], a[0], bs, cs, h0_f, need_y)
    y_b, h_b = ssd_scan(xs[:, ::-1], dt[:, ::-1, 1], a[1], bs[:, ::-1], cs[:, ::-1], h0_b, need_y)
    y = (y_f + y_b[:, ::-1]) if need_y else None
    return xs, y, h_f, h_b


def ssd_output(xs, y, z, d_skip, g):
    bsz, n = z.shape[:2]
    y = y + d_skip.astype(jnp.float32)[:, None] * xs.astype(jnp.float32)
    y = y.reshape(bsz, n, SSD_WIDTH) * jax.nn.silu(z.astype(jnp.float32))
    return rms_norm(y, g).astype(z.dtype)


def conv_module(glu_in, dw_w, dw_b, ln_g, ln_b):
    a, gt = jnp.split(glu_in, 2, axis=-1)
    u = a * jax.nn.sigmoid(gt)
    u = dwconv_centred(u, dw_w, dw_b)
    return jax.nn.silu(layer_norm(u, ln_g, ln_b))


def hybrid_layer(lat, cx, c, c_ctx, p, cos, sin, ctx_out):
    ml = jnp.split((jax.nn.silu(c) @ p['w_mod'] + p['b_mod'])[:, None, :], N_MOD, axis=-1)
    mc = jnp.split(jax.nn.silu(c_ctx) @ p['w_mod'] + p['b_mod'], N_MOD, axis=-1)

    def sub_in(s, m, j):
        return modulate(rms_norm(s, p['norm_g'][j]), m[3 * j], m[3 * j + 1])

    def ffn_res(s, m, j, f):
        return s + 0.5 * m[3 * j + 2] * swiglu(sub_in(s, m, j), p['w_ffn_in'][f], p['w_ffn_out'][f])

    lat = ffn_res(lat, ml, 0, 0)
    cx = ffn_res(cx, mc, 0, 0)

    q_l, k_l, v_l, z_l, xbc_l, dt_l, glu_l = split_cols(sub_in(lat, ml, 1) @ p['w_in'])
    q_c, k_c, v_c, z_c, xbc_c, dt_c, glu_c = split_cols(sub_in(cx, mc, 1) @ p['w_in'])

    k_c = rms_norm(to_heads(k_c, ATTN_KV_HEADS), p['qk_g'][1])
    v_c = to_heads(v_c, ATTN_KV_HEADS)
    q_l = apply_rope(rms_norm(to_heads(q_l, ATTN_HEADS), p['qk_g'][0]), cos, sin)
    k_l = apply_rope(rms_norm(to_heads(k_l, ATTN_KV_HEADS), p['qk_g'][1]), cos, sin)
    attn_l = attend(q_l, jnp.concatenate([k_c, k_l], axis=1),
                    jnp.concatenate([v_c, to_heads(v_l, ATTN_KV_HEADS)], axis=1))

    h0 = jnp.zeros((cx.shape[0], SSD_HEADS, SSD_HEAD_DIM, SSD_STATE), jnp.float32)
    ssd_p = (p['ssd_conv_w'], p['ssd_conv_b'], p['ssd_dt_bias'], p['ssd_a_log'])
    xs_c, y_c, hf_c, hb_c = ssd_side(xbc_c, dt_c, *ssd_p, h0, h0, ctx_out)
    xs_l, y_l, _, _ = ssd_side(xbc_l, dt_l, *ssd_p, hf_c, hb_c, True)
    ssd_l = ssd_output(xs_l, y_l, z_l, p['ssd_d'], p['ssd_norm_g'])

    cm_l = conv_module(glu_l, p['cm_dw_w'], p['cm_dw_b'], p['cm_ln_g'], p['cm_ln_b'])

    lat = lat + ml[5] * (jnp.concatenate([attn_l, ssd_l, cm_l], axis=-1) @ p['w_out'])
    lat = ffn_res(lat, ml, 2, 1)
    if not ctx_out:
        return lat, None

    q_c = rms_norm(to_heads(q_c, ATTN_HEADS), p['qk_g'][0])
    attn_c = attend(q_c, k_c, v_c)
    ssd_c = ssd_output(xs_c, y_c, z_c, p['ssd_d'], p['ssd_norm_g'])
    cm_c = conv_module(glu_c, p['cm_dw_w'], p['cm_dw_b'], p['cm_ln_g'], p['cm_ln_b'])
    cx = cx + mc[5] * (jnp.concatenate([attn_c, ssd_c, cm_c], axis=-1) @ p['w_out'])
    cx = ffn_res(cx, mc, 2, 1)
    return lat, cx


def setup_inputs(seed: int = 0) -> dict:
    key = jax.random.key(seed)
    ks = jax.random.split(key, 24)
    f32 = jnp.float32

    def nrm(k, shape, scale):
        return jax.random.normal(k, shape, f32) * scale

    def gain(k, shape):
        return 1.0 + nrm(k, shape, 0.05)

    dt = jnp.exp(jax.random.uniform(ks[13], (DEPTH, 2, SSD_HEADS), f32, math.log(DT_MIN), math.log(DT_MAX)))
    dt_bias = dt + jnp.log(-jnp.expm1(-dt))
    a_log = jnp.log(jax.random.uniform(ks[14], (DEPTH, 2, SSD_HEADS), f32, 1.0, 16.0))
    return {
        'x': nrm(ks[0], (BATCH, SEQ, D_MODEL), 1.0),
        'c': nrm(ks[1], (BATCH, D_MODEL), 1.0),
        'ctx': nrm(ks[2], (BATCH, CTX_LEN, D_MODEL), 1.0),
        'c_ctx': nrm(ks[3], (D_MODEL,), 1.0),
        'w_mod': nrm(ks[4], (DEPTH, D_MODEL, N_MOD * D_MODEL), 0.5 * D_MODEL ** -0.5),
        'b_mod': nrm(ks[5], (DEPTH, N_MOD * D_MODEL), 0.02),
        'norm_g': gain(ks[6], (DEPTH, 3, D_MODEL)),
        'w_ffn_in': nrm(ks[7], (DEPTH, 2, D_MODEL, 2 * D_FF), D_MODEL ** -0.5),
        'w_ffn_out': nrm(ks[8], (DEPTH, 2, D_FF, D_MODEL), D_FF ** -0.5),
        'w_in': nrm(ks[9], (DEPTH, D_MODEL, IN_COLS), D_MODEL ** -0.5),
        'w_out': nrm(ks[10], (DEPTH, D_MIX, D_MODEL), D_MIX ** -0.5),
        'qk_g': gain(ks[11], (DEPTH, 2, HEAD_DIM)),
        'ssd_conv_w': nrm(ks[12], (DEPTH, SSD_CONV, SSD_CONV_CH), SSD_CONV ** -0.5),
        'ssd_conv_b': nrm(ks[15], (DEPTH, SSD_CONV_CH), 0.02),
        'ssd_dt_bias': dt_bias,
        'ssd_a_log': a_log,
        'ssd_d': 1.0 + nrm(ks[16], (DEPTH, SSD_HEADS), 0.1),
        'ssd_norm_g': gain(ks[17], (DEPTH, SSD_WIDTH)),
        'cm_dw_w': nrm(ks[18], (DEPTH, CM_KERNEL, CM_WIDTH), CM_KERNEL ** -0.5),
        'cm_dw_b': nrm(ks[19], (DEPTH, CM_WIDTH), 0.02),
        'cm_ln_g': gain(ks[20], (DEPTH, CM_WIDTH)),
        'cm_ln_b': nrm(ks[21], (DEPTH, CM_WIDTH), 0.02),
        'final_g': gain(ks[22], (D_MODEL,)),
    }


def reference(x, c, ctx, c_ctx, w_mod, b_mod, norm_g, w_ffn_in, w_ffn_out, w_in, w_out, qk_g,
              ssd_conv_w, ssd_conv_b, ssd_dt_bias, ssd_a_log, ssd_d, ssd_norm_g,
              cm_dw_w, cm_dw_b, cm_ln_g, cm_ln_b, final_g):
    cos, sin = rope_tables(x.shape[1])
    lat, cx = x, ctx
    for l in range(DEPTH):
        p = {
            'w_mod': w_mod[l], 'b_mod': b_mod[l], 'norm_g': norm_g[l],
            'w_ffn_in': w_ffn_in[l], 'w_ffn_out': w_ffn_out[l],
            'w_in': w_in[l], 'w_out': w_out[l], 'qk_g': qk_g[l],
            'ssd_conv_w': ssd_conv_w[l], 'ssd_conv_b': ssd_conv_b[l],
            'ssd_dt_bias': ssd_dt_bias[l], 'ssd_a_log': ssd_a_log[l],
            'ssd_d': ssd_d[l], 'ssd_norm_g': ssd_norm_g[l],
            'cm_dw_w': cm_dw_w[l], 'cm_dw_b': cm_dw_b[l],
            'cm_ln_g': cm_ln_g[l], 'cm_ln_b': cm_ln_b[l],
        }
        lat, cx = hybrid_layer(lat, cx, c, c_ctx, p, cos, sin, l < DEPTH - 1)
    return rms_norm(lat, final_g)
```

```python
import functools
import math

import jax
import jax.numpy as jnp
from jax import lax
from jax.experimental import pallas as pl
from jax.experimental.pallas import tpu as pltpu

F32 = jnp.float32
BF16 = jnp.bfloat16

D_MODEL = 2048
DEPTH = 2
GRID_W = 64
N_MOD = 9
D_FF = 5632
EPS = 1e-6

HEAD_DIM = 128
ATTN_HEADS = 8
ATTN_KV_HEADS = 2
ATTN_GROUPS = ATTN_HEADS // ATTN_KV_HEADS
ATTN_WIDTH = ATTN_HEADS * HEAD_DIM
KV_WIDTH = ATTN_KV_HEADS * HEAD_DIM
ROPE_THETA = 10000.0
ROPE_AXIS_DIM = HEAD_DIM // 2

SSD_HEADS = 8
SSD_HEAD_DIM = 64
SSD_WIDTH = SSD_HEADS * SSD_HEAD_DIM
SSD_GROUPS = 2
SSD_STATE = 128
SSD_CONV = 7
SSD_CHUNK = 128
SSD_CONV_CH = SSD_WIDTH + 2 * SSD_GROUPS * SSD_STATE

CM_WIDTH = 512
CM_KERNEL = 31
D_MIX = ATTN_WIDTH + SSD_WIDTH + CM_WIDTH

LANES = 128
SUBLANES = 8
VMEM_LIMIT = 56 * 1024 * 1024

COL_Q = 0
COL_K = COL_Q + ATTN_WIDTH
COL_V = COL_K + KV_WIDTH
COL_Z = COL_V + KV_WIDTH
COL_XBC = COL_Z + SSD_WIDTH
COL_GLU = COL_XBC + SSD_CONV_CH
COL_DT = COL_GLU + 2 * CM_WIDTH
P_COLS = COL_DT + 2 * LANES

ROW_TILE = 256
MM_TILE = 640
FF_CHUNK = 256
KV_TILE = 640
Q_TILE = 256
SSD_HALO = 8
CM_HALO = 16


def _cparams(sem):
    return pltpu.CompilerParams(dimension_semantics=sem, vmem_limit_bytes=VMEM_LIMIT)


def _sigmoid(x):
    return 1.0 / (1.0 + jnp.exp(-x))


def _silu(x):
    return x * _sigmoid(x)


def _rms(x, g):
    return x * lax.rsqrt(jnp.mean(x * x, axis=-1, keepdims=True) + EPS) * g


def _mod_rows(mod_ref, k, row0, rows, n_lat):
    ridx = row0 + lax.broadcasted_iota(jnp.int32, (rows, 1), 0)
    return jnp.where(ridx < n_lat, mod_ref[k, 1:2, :], mod_ref[k, 0:1, :])


ROW_CHUNK = 128


def _norm_mod_store(h_scr, x_ref, mod_ref, g_ref, row0, tm, n_lat):
    for r in range(0, tm, ROW_CHUNK):
        shift = _mod_rows(mod_ref, 0, row0 + r, ROW_CHUNK, n_lat)
        scale = _mod_rows(mod_ref, 1, row0 + r, ROW_CHUNK, n_lat)
        h = _rms(x_ref[r:r + ROW_CHUNK, :], g_ref[...]) * (1.0 + scale) + shift
        h_scr[r:r + ROW_CHUNK, :] = h.astype(BF16)


def _mod_kernel(c_ref, w_ref, b_ref, o_ref):
    a = _silu(c_ref[...]).astype(BF16)
    o_ref[0] = jnp.dot(a, w_ref[0].astype(BF16), preferred_element_type=F32) + b_ref[0]


def _modulation(cvec, w_mod, b_mod):
    depth, d, n = w_mod.shape
    tn = 1024
    return pl.pallas_call(
        _mod_kernel,
        grid=(depth, n // tn),
        in_specs=[
            pl.BlockSpec((SUBLANES, d), lambda l, j: (0, 0)),
            pl.BlockSpec((1, d, tn), lambda l, j: (l, 0, j)),
            pl.BlockSpec((1, 1, tn), lambda l, j: (l, 0, j)),
        ],
        out_specs=pl.BlockSpec((1, SUBLANES, tn), lambda l, j: (l, 0, j)),
        out_shape=jax.ShapeDtypeStruct((depth, SUBLANES, n), F32),
        compiler_params=_cparams(("arbitrary", "arbitrary")),
        name="modulation",
    )(cvec, w_mod, b_mod.reshape(depth, 1, n))


def _ffn_kernel(x_ref, mod_ref, g_ref, wg_ref, wu_ref, wo_ref, *rest, n_lat, tm, final):
    if final:
        fg_ref, o_ref, h_scr, acc_scr = rest
    else:
        o_ref, h_scr, acc_scr = rest
    i = pl.program_id(0)
    f = pl.program_id(1)

    @pl.when(f == 0)
    def _():
        _norm_mod_store(h_scr, x_ref, mod_ref, g_ref, i * tm, tm, n_lat)
        acc_scr[...] = jnp.zeros_like(acc_scr)

    h = h_scr[...]
    gt = jnp.dot(h, wg_ref[...], preferred_element_type=F32)
    up = jnp.dot(h, wu_ref[...], preferred_element_type=F32)
    a = (_silu(gt) * up).astype(BF16)
    acc_scr[...] += jnp.dot(a, wo_ref[...], preferred_element_type=F32)

    @pl.when(f == pl.num_programs(1) - 1)
    def _():
        for r in range(0, tm, ROW_CHUNK):
            rs = slice(r, r + ROW_CHUNK)
            gate = _mod_rows(mod_ref, 2, i * tm + r, ROW_CHUNK, n_lat)
            y = x_ref[rs, :] + (0.5 * gate) * acc_scr[rs, :]
            if final:
                y = _rms(y, fg_ref[...])
            o_ref[rs, :] = y


def _ffn(xs, mod3, norm_g, w_in, w_out, *, rows, tm, n_lat, final_g=None):
    d = xs.shape[1]
    nf = D_FF // FF_CHUNK
    final = final_g is not None
    in_specs = [
        pl.BlockSpec((tm, d), lambda i, f: (i, 0)),
        pl.BlockSpec((3, SUBLANES, d), lambda i, f: (0, 0, 0)),
        pl.BlockSpec((1, d), lambda i, f: (0, 0)),
        pl.BlockSpec((d, FF_CHUNK), lambda i, f: (0, f)),
        pl.BlockSpec((d, FF_CHUNK), lambda i, f: (0, nf + f)),
        pl.BlockSpec((FF_CHUNK, d), lambda i, f: (f, 0)),
    ]
    args = [xs, mod3, norm_g.reshape(1, d), w_in, w_in, w_out]
    if final:
        in_specs.append(pl.BlockSpec((1, d), lambda i, f: (0, 0)))
        args.append(final_g.reshape(1, d))
    return pl.pallas_call(
        functools.partial(_ffn_kernel, n_lat=n_lat, tm=tm, final=final),
        grid=(rows // tm, nf),
        in_specs=in_specs,
        out_specs=pl.BlockSpec((tm, d), lambda i, f: (i, 0)),
        out_shape=jax.ShapeDtypeStruct((rows, d), F32),
        scratch_shapes=[pltpu.VMEM((tm, d), BF16), pltpu.VMEM((tm, d), F32)],
        compiler_params=_cparams(("parallel", "arbitrary")),
        name="ffn",
    )(*args)


def _proj_kernel(x_ref, mod_ref, g_ref, w_ref, o_ref, h_scr, *, n_lat, tm):
    i = pl.program_id(0)

    @pl.when(pl.program_id(1) == 0)
    def _():
        _norm_mod_store(h_scr, x_ref, mod_ref, g_ref, i * tm, tm, n_lat)

    o_ref[...] = jnp.dot(h_scr[...], w_ref[...], preferred_element_type=F32)


def _in_proj(xs, mod3, norm_g, w_in_r, *, tm, n_lat):
    n, d = xs.shape
    tn = P_COLS // 2
    return pl.pallas_call(
        functools.partial(_proj_kernel, n_lat=n_lat, tm=tm),
        grid=(n // tm, P_COLS // tn),
        in_specs=[
            pl.BlockSpec((tm, d), lambda i, j: (i, 0)),
            pl.BlockSpec((3, SUBLANES, d), lambda i, j: (0, 0, 0)),
            pl.BlockSpec((1, d), lambda i, j: (0, 0)),
            pl.BlockSpec((d, tn), lambda i, j: (0, j)),
        ],
        out_specs=pl.BlockSpec((tm, tn), lambda i, j: (i, j)),
        out_shape=jax.ShapeDtypeStruct((n, P_COLS), F32),
        scratch_shapes=[pltpu.VMEM((tm, d), BF16)],
        compiler_params=_cparams(("parallel", "arbitrary")),
        name="in_proj",
    )(xs, mod3, norm_g.reshape(1, d), w_in_r)


def _qkv_kernel(q_ref, kv_ref, cos_ref, sin_ref, g_ref, qo_ref, ko_ref, vo_ref):
    cos = cos_ref[...]
    sin = sin_ref[...]
    lane = lax.broadcasted_iota(jnp.int32, cos.shape, 1)
    first_half = (lane % ROPE_AXIS_DIM) < (ROPE_AXIS_DIM // 2)

    def norm_rope(xh, g):
        xh = _rms(xh, g)
        swapped = jnp.where(first_half, pltpu.roll(xh, LANES - 32, 1), pltpu.roll(xh, 32, 1))
        return (xh * cos + swapped * sin).astype(BF16)

    for h in range(ATTN_HEADS):
        sl = slice(h * HEAD_DIM, (h + 1) * HEAD_DIM)
        qo_ref[:, sl] = norm_rope(q_ref[:, sl], g_ref[0:1, :])
    for h in range(ATTN_KV_HEADS):
        sl = slice(h * HEAD_DIM, (h + 1) * HEAD_DIM)
        ko_ref[:, sl] = norm_rope(kv_ref[:, sl], g_ref[1:2, :])
    vo_ref[...] = kv_ref[:, KV_WIDTH:].astype(BF16)


def _qkv_prep(p, cos, sin, qk_g):
    n = p.shape[0]
    t = ROW_TILE
    return pl.pallas_call(
        _qkv_kernel,
        grid=(n // t,),
        in_specs=[
            pl.BlockSpec((t, ATTN_WIDTH), lambda i: (i, COL_Q // ATTN_WIDTH)),
            pl.BlockSpec((t, 2 * KV_WIDTH), lambda i: (i, COL_K // (2 * KV_WIDTH))),
            pl.BlockSpec((t, HEAD_DIM), lambda i: (i, 0)),
            pl.BlockSpec((t, HEAD_DIM), lambda i: (i, 0)),
            pl.BlockSpec((2, HEAD_DIM), lambda i: (0, 0)),
        ],
        out_specs=[
            pl.BlockSpec((t, ATTN_WIDTH), lambda i: (i, 0)),
            pl.BlockSpec((t, KV_WIDTH), lambda i: (i, 0)),
            pl.BlockSpec((t, KV_WIDTH), lambda i: (i, 0)),
        ],
        out_shape=[
            jax.ShapeDtypeStruct((n, ATTN_WIDTH), BF16),
            jax.ShapeDtypeStruct((n, KV_WIDTH), BF16),
            jax.ShapeDtypeStruct((n, KV_WIDTH), BF16),
        ],
        compiler_params=_cparams(("parallel",)),
        name="qkv_prep",
    )(p, p, cos, sin, qk_g)


def _flash_kernel(*refs, aliased):
    if aliased:
        q_ref, k_ref, v_ref, _, o_ref, m_scr, l_scr, acc_scr = refs
    else:
        q_ref, k_ref, v_ref, o_ref, m_scr, l_scr, acc_scr = refs
    j = pl.program_id(1)
    tq = q_ref.shape[0]
    c = (HEAD_DIM ** -0.5) * math.log2(math.e)

    @pl.when(j == 0)
    def _():
        m_scr[...] = jnp.full_like(m_scr, -jnp.inf)
        l_scr[...] = jnp.zeros_like(l_scr)
        acc_scr[...] = jnp.zeros_like(acc_scr)

    for g in range(ATTN_KV_HEADS):
        q = jnp.concatenate(
            [q_ref[:, (ATTN_GROUPS * g + h) * HEAD_DIM:(ATTN_GROUPS * g + h + 1) * HEAD_DIM]
             for h in range(ATTN_GROUPS)], axis=0)
        k = k_ref[:, g * HEAD_DIM:(g + 1) * HEAD_DIM]
        v = v_ref[:, g * HEAD_DIM:(g + 1) * HEAD_DIM]
        s = lax.dot_general(q, k, (((1,), (1,)), ((), ())), preferred_element_type=F32)
        m_prev = m_scr[g]
        m_new = jnp.maximum(m_prev, jnp.max(s, axis=1, keepdims=True))
        alpha = jnp.exp2((m_prev - m_new) * c)
        p = jnp.exp2((s - m_new[:, 0:1]) * c)
        l_scr[g] = alpha * l_scr[g] + jnp.sum(p, axis=1, keepdims=True)
        acc_scr[g] = alpha * acc_scr[g] + jnp.dot(p.astype(BF16), v, preferred_element_type=F32)
        m_scr[g] = m_new

    @pl.when(j == pl.num_programs(1) - 1)
    def _():
        for g in range(ATTN_KV_HEADS):
            o = acc_scr[g] / l_scr[g]
            for h in range(ATTN_GROUPS):
                col = (ATTN_GROUPS * g + h) * HEAD_DIM
                o_ref[:, col:col + HEAD_DIM] = o[h * tq:(h + 1) * tq, :].astype(o_ref.dtype)


def _attention(q, k, v, *, q_tiles, q_off, kv_tiles, kv_off, tq, tk, prev=None):
    n = q.shape[0]
    aliased = prev is not None
    in_specs = [
        pl.BlockSpec((tq, ATTN_WIDTH), lambda i, j: (q_off + i, 0)),
        pl.BlockSpec((tk, KV_WIDTH), lambda i, j: (kv_off + j, 0)),
        pl.BlockSpec((tk, KV_WIDTH), lambda i, j: (kv_off + j, 0)),
    ]
    args = [q, k, v]
    if aliased:
        in_specs.append(pl.BlockSpec(memory_space=pl.ANY))
        args.append(prev)
    rows = ATTN_GROUPS * tq
    return pl.pallas_call(
        functools.partial(_flash_kernel, aliased=aliased),
        grid=(q_tiles, kv_tiles),
        in_specs=in_specs,
        out_specs=pl.BlockSpec((tq, ATTN_WIDTH), lambda i, j: (q_off + i, 0)),
        out_shape=jax.ShapeDtypeStruct((n, ATTN_WIDTH), BF16),
        scratch_shapes=[
            pltpu.VMEM((ATTN_KV_HEADS, rows, HEAD_DIM), F32),
            pltpu.VMEM((ATTN_KV_HEADS, rows, HEAD_DIM), F32),
            pltpu.VMEM((ATTN_KV_HEADS, rows, HEAD_DIM), F32),
        ],
        input_output_aliases={3: 0} if aliased else {},
        compiler_params=_cparams(("parallel", "arbitrary")),
        name="attention_ctx" if aliased else "attention",
    )(*args)


def _segment_edges(i, lat_tiles):
    first = jnp.logical_or(i == 0, i == lat_tiles)
    last = jnp.logical_or(i == lat_tiles - 1, i == pl.num_programs(0) - 1)
    return first, last


def _ssd_conv_kernel(prev_ref, cur_ref, next_ref, w_ref, b_ref, o_ref, ext_scr, *, lat_tiles):
    i = pl.program_id(0)
    t = cur_ref.shape[0]
    first, last = _segment_edges(i, lat_tiles)
    ext_scr[0:SSD_HALO, :] = jnp.where(first, 0.0, prev_ref[...])
    ext_scr[SSD_HALO:SSD_HALO + t, :] = cur_ref[...]
    ext_scr[SSD_HALO + t:, :] = jnp.where(last, 0.0, next_ref[...])
    pad = SSD_CONV // 2
    acc = jnp.broadcast_to(b_ref[...], cur_ref.shape)
    for k in range(SSD_CONV):
        acc = acc + w_ref[k:k + 1, :] * ext_scr[pl.ds(SSD_HALO - pad + k, t), :]
    o_ref[...] = _silu(acc)


def _ssd_conv(p, conv_w, conv_b, *, n_lat):
    n = p.shape[0]
    t = ROW_TILE
    hb = t // SSD_HALO
    last_hb = n // SSD_HALO - 1
    cb = COL_XBC // SSD_CONV_CH
    return pl.pallas_call(
        functools.partial(_ssd_conv_kernel, lat_tiles=n_lat // t),
        grid=(n // t,),
        in_specs=[
            pl.BlockSpec((SSD_HALO, SSD_CONV_CH), lambda i: (jnp.maximum(i * hb - 1, 0), cb)),
            pl.BlockSpec((t, SSD_CONV_CH), lambda i: (i, cb)),
            pl.BlockSpec((SSD_HALO, SSD_CONV_CH), lambda i: (jnp.minimum((i + 1) * hb, last_hb), cb)),
            pl.BlockSpec((SSD_CONV, SSD_CONV_CH), lambda i: (0, 0)),
            pl.BlockSpec((1, SSD_CONV_CH), lambda i: (0, 0)),
        ],
        out_specs=pl.BlockSpec((t, SSD_CONV_CH), lambda i: (i, 0)),
        out_shape=jax.ShapeDtypeStruct((n, SSD_CONV_CH), F32),
        scratch_shapes=[pltpu.VMEM((t + 2 * SSD_HALO, SSD_CONV_CH), F32)],
        compiler_params=_cparams(("parallel",)),
        name="ssd_conv",
    )(p, p, p, conv_w, conv_b.reshape(1, SSD_CONV_CH))


def _cm_kernel(ap_ref, gp_ref, a_ref, g_ref, an_ref, gn_ref, w_ref, b_ref, lg_ref, lb_ref, o_ref, ext_scr,
               *, lat_tiles):
    i = pl.program_id(0)
    t = a_ref.shape[0]
    first, last = _segment_edges(i, lat_tiles)

    def glu(a, gt):
        return a[...] * _sigmoid(gt[...])

    ext_scr[0:CM_HALO, :] = jnp.where(first, 0.0, glu(ap_ref, gp_ref))
    ext_scr[CM_HALO:CM_HALO + t, :] = glu(a_ref, g_ref)
    ext_scr[CM_HALO + t:, :] = jnp.where(last, 0.0, glu(an_ref, gn_ref))
    pad = CM_KERNEL // 2
    acc = jnp.broadcast_to(b_ref[...], a_ref.shape)
    for k in range(CM_KERNEL):
        acc = acc + w_ref[k:k + 1, :] * ext_scr[pl.ds(CM_HALO - pad + k, t), :]
    mu = jnp.mean(acc, axis=-1, keepdims=True)
    cen = acc - mu
    var = jnp.mean(cen * cen, axis=-1, keepdims=True)
    y = cen * lax.rsqrt(var + EPS) * lg_ref[...] + lb_ref[...]
    o_ref[...] = _silu(y).astype(o_ref.dtype)


def _conv_module(p, dw_w, dw_b, ln_g, ln_b, *, n_lat):
    n = p.shape[0]
    t = ROW_TILE
    hb = t // CM_HALO
    last_hb = n // CM_HALO - 1
    ca = COL_GLU // CM_WIDTH
    cg = ca + 1

    def prev_map(c):
        return lambda i: (jnp.maximum(i * hb - 1, 0), c)

    def next_map(c):
        return lambda i: (jnp.minimum((i + 1) * hb, last_hb), c)

    vec = pl.BlockSpec((1, CM_WIDTH), lambda i: (0, 0))
    return pl.pallas_call(
        functools.partial(_cm_kernel, lat_tiles=n_lat // t),
        grid=(n // t,),
        in_specs=[
            pl.BlockSpec((CM_HALO, CM_WIDTH), prev_map(ca)),
            pl.BlockSpec((CM_HALO, CM_WIDTH), prev_map(cg)),
            pl.BlockSpec((t, CM_WIDTH), lambda i: (i, ca)),
            pl.BlockSpec((t, CM_WIDTH), lambda i: (i, cg)),
            pl.BlockSpec((CM_HALO, CM_WIDTH), next_map(ca)),
            pl.BlockSpec((CM_HALO, CM_WIDTH), next_map(cg)),
            pl.BlockSpec((CM_KERNEL, CM_WIDTH), lambda i: (0, 0)),
            vec, vec, vec,
        ],
        out_specs=pl.BlockSpec((t, CM_WIDTH), lambda i: (i, 0)),
        out_shape=jax.ShapeDtypeStruct((n, CM_WIDTH), BF16),
        scratch_shapes=[pltpu.VMEM((t + 2 * CM_HALO, CM_WIDTH), F32)],
        compiler_params=_cparams(("parallel",)),
        name="conv_module",
    )(p, p, p, p, p, p, dw_w, dw_b.reshape(1, -1), ln_g.reshape(1, -1), ln_b.reshape(1, -1))


def _split3_dot(tri, x):
    hi = x.astype(BF16)
    r1 = x - hi.astype(F32)
    mid = r1.astype(BF16)
    lo = (r1 - mid.astype(F32)).astype(BF16)
    return (jnp.dot(tri, hi, preferred_element_type=F32) + jnp.dot(tri, mid, preferred_element_type=F32)
            + jnp.dot(tri, lo, preferred_element_type=F32))


def _ssd_scan_kernel(xa_ref, dt_ref, dtb_ref, alog_ref, y_ref, h_scr):
    d = pl.program_id(0)
    q = SSD_CHUNK

    @pl.when(pl.program_id(1) == 0)
    def _():
        h_scr[...] = jnp.zeros_like(h_scr)

    x = dt_ref[...] + dtb_ref[0]
    dt = jnp.maximum(x, 0.0) + jnp.log1p(jnp.exp(-jnp.abs(x)))
    da = dt * (-jnp.exp(alog_ref[0]))
    ii = lax.broadcasted_iota(jnp.int32, (q, q), 0)
    jj = lax.broadcasted_iota(jnp.int32, (q, q), 1)
    lane = jj
    inc = ((jj - ii) * (1 - 2 * d)) <= 0
    tri = jnp.where(inc, 1.0, 0.0).astype(BF16)
    cum = _split3_dot(tri, da)
    tot = jnp.sum(da, axis=0, keepdims=True)
    cum_t = cum.T
    dt_t = dt.T
    e_in = jnp.exp(cum)
    w_out = jnp.exp(tot - cum) * dt
    e_tot = jnp.exp(tot)

    xa = xa_ref[...]
    pair = 2 * SSD_HEAD_DIM
    low = lane < SSD_HEAD_DIM
    for g in range(SSD_GROUPS):
        b = xa[:, SSD_WIDTH + g * SSD_STATE:SSD_WIDTH + (g + 1) * SSD_STATE].astype(BF16)
        cg = xa[:, SSD_WIDTH + (SSD_GROUPS + g) * SSD_STATE:
                SSD_WIDTH + (SSD_GROUPS + g + 1) * SSD_STATE].astype(BF16)
        cb = lax.dot_general(cg, b, (((1,), (1,)), ((), ())), preferred_element_type=F32)
        for m in range(SSD_HEADS // SSD_GROUPS // 2):
            pi = g * (SSD_HEADS // SSD_GROUPS // 2) + m
            h0, h1 = 2 * pi, 2 * pi + 1
            xp = xa[:, pi * pair:(pi + 1) * pair]
            xp_b = xp.astype(BF16)
            ys = []
            for hd in (h0, h1):
                seg = cum[:, hd:hd + 1] - cum_t[hd:hd + 1, :]
                dec = jnp.exp(jnp.where(inc, seg, -jnp.inf))
                sc = (cb * dec * dt_t[hd:hd + 1, :]).astype(BF16)
                ys.append(jnp.dot(sc, xp_b, preferred_element_type=F32))
            y_diag = jnp.where(low, ys[0], ys[1])
            hp = h_scr[pi]
            y_off = lax.dot_general(cg, hp.astype(BF16), (((1,), (1,)), ((), ())),
                                    preferred_element_type=F32)
            y_off = y_off * jnp.where(low, e_in[:, h0:h0 + 1], e_in[:, h1:h1 + 1])
            y_ref[0, :, pi * pair:(pi + 1) * pair] = y_diag + y_off
            xw = xp * jnp.where(low, w_out[:, h0:h0 + 1], w_out[:, h1:h1 + 1])
            st = jnp.dot(xw.T.astype(BF16), b, preferred_element_type=F32)
            row_low = ii < SSD_HEAD_DIM
            keep = jnp.where(row_low, e_tot[:, h0:h0 + 1], e_tot[:, h1:h1 + 1])
            h_scr[pi] = keep * hp + st


def _ssd_scan(xa, p, dt_bias, a_log, *, n_lat):
    n = xa.shape[0]
    nc = n // SSD_CHUNK
    lat_c = n_lat // SSD_CHUNK
    ctx_c = nc - lat_c
    dt_cb = COL_DT // LANES

    def chunk(d, s):
        fwd = jnp.where(s < ctx_c, lat_c + s, s - ctx_c)
        bwd = jnp.where(s < ctx_c, nc - 1 - s, nc - 1 - s)
        return jnp.where(d == 0, fwd, bwd)

    pad = LANES - SSD_HEADS
    dtb = jnp.pad(dt_bias, ((0, 0), (0, pad))).reshape(2, 1, LANES)
    alog = jnp.pad(a_log, ((0, 0), (0, pad))).reshape(2, 1, LANES)
    return pl.pallas_call(
        _ssd_scan_kernel,
        grid=(2, nc),
        in_specs=[
            pl.BlockSpec((SSD_CHUNK, SSD_CONV_CH), lambda d, s: (chunk(d, s), 0)),
            pl.BlockSpec((SSD_CHUNK, LANES), lambda d, s: (chunk(d, s), dt_cb + d)),
            pl.BlockSpec((1, 1, LANES), lambda d, s: (d, 0, 0)),
            pl.BlockSpec((1, 1, LANES), lambda d, s: (d, 0, 0)),
        ],
        out_specs=pl.BlockSpec((1, SSD_CHUNK, SSD_WIDTH), lambda d, s: (d, chunk(d, s), 0)),
        out_shape=jax.ShapeDtypeStruct((2, n, SSD_WIDTH), F32),
        scratch_shapes=[pltpu.VMEM((SSD_HEADS // 2, 2 * SSD_HEAD_DIM, SSD_STATE), F32)],
        compiler_params=_cparams(("arbitrary", "arbitrary")),
        name="ssd_scan",
    )(xa, p, dtb, alog)


def _ssd_out_kernel(y_ref, xs_ref, z_ref, d_ref, g_ref, o_ref):
    y = y_ref[0] + y_ref[1] + d_ref[...] * xs_ref[...]
    y = y * _silu(z_ref[...])
    o_ref[...] = _rms(y, g_ref[...]).astype(o_ref.dtype)


def _ssd_out(y, xa, p, d_skip, norm_g):
    n = xa.shape[0]
    t = ROW_TILE
    vec = pl.BlockSpec((1, SSD_WIDTH), lambda i: (0, 0))
    return pl.pallas_call(
        _ssd_out_kernel,
        grid=(n // t,),
        in_specs=[
            pl.BlockSpec((2, t, SSD_WIDTH), lambda i: (0, i, 0)),
            pl.BlockSpec((t, SSD_WIDTH), lambda i: (i, 0)),
            pl.BlockSpec((t, SSD_WIDTH), lambda i: (i, COL_Z // SSD_WIDTH)),
            vec, vec,
        ],
        out_specs=pl.BlockSpec((t, SSD_WIDTH), lambda i: (i, 0)),
        out_shape=jax.ShapeDtypeStruct((n, SSD_WIDTH), BF16),
        compiler_params=_cparams(("parallel",)),
        name="ssd_out",
    )(y, xa, p, jnp.repeat(d_skip, SSD_HEAD_DIM).reshape(1, SSD_WIDTH), norm_g.reshape(1, SSD_WIDTH))


def _out_proj_kernel(x_ref, a_ref, s_ref, c_ref, mod_ref, w_ref, o_ref, *, n_lat, tm):
    i = pl.program_id(0)
    acc = jnp.dot(a_ref[...], w_ref[0:ATTN_WIDTH, :], preferred_element_type=F32)
    acc += jnp.dot(s_ref[...], w_ref[ATTN_WIDTH:ATTN_WIDTH + SSD_WIDTH, :], preferred_element_type=F32)
    acc += jnp.dot(c_ref[...], w_ref[ATTN_WIDTH + SSD_WIDTH:, :], preferred_element_type=F32)
    gate = _mod_rows(mod_ref, 2, i * tm, tm, n_lat)
    o_ref[...] = x_ref[...] + gate * acc


def _out_proj(xs, attn, ssd, cm, mod3, w_out, *, rows, tm, n_lat):
    d = xs.shape[1]
    return pl.pallas_call(
        functools.partial(_out_proj_kernel, n_lat=n_lat, tm=tm),
        grid=(rows // tm,),
        in_specs=[
            pl.BlockSpec((tm, d), lambda i: (i, 0)),
            pl.BlockSpec((tm, ATTN_WIDTH), lambda i: (i, 0)),
            pl.BlockSpec((tm, SSD_WIDTH), lambda i: (i, 0)),
            pl.BlockSpec((tm, CM_WIDTH), lambda i: (i, 0)),
            pl.BlockSpec((3, SUBLANES, d), lambda i: (0, 0, 0)),
            pl.BlockSpec((D_MIX, d), lambda i: (0, 0)),
        ],
        out_specs=pl.BlockSpec((tm, d), lambda i: (i, 0)),
        out_shape=jax.ShapeDtypeStruct((rows, d), F32),
        compiler_params=_cparams(("parallel",)),
        name="out_proj",
    )(xs, attn, ssd, cm, mod3, w_out)


def _rope_tables(n_lat, n):
    rows = n_lat // GRID_W
    row = jnp.repeat(jnp.arange(rows), GRID_W).astype(F32)
    col = jnp.tile(jnp.arange(GRID_W), rows).astype(F32)
    inv = ROPE_THETA ** (-jnp.arange(0, ROPE_AXIS_DIM, 2, dtype=F32) / ROPE_AXIS_DIM)
    ar = row[:, None] * inv
    ac = col[:, None] * inv
    cos = jnp.concatenate([jnp.cos(ar), jnp.cos(ar), jnp.cos(ac), jnp.cos(ac)], axis=1)
    sin = jnp.concatenate([-jnp.sin(ar), jnp.sin(ar), -jnp.sin(ac), jnp.sin(ac)], axis=1)
    pad = n - n_lat
    cos = jnp.concatenate([cos, jnp.ones((pad, HEAD_DIM), F32)], axis=0)
    sin = jnp.concatenate([sin, jnp.zeros((pad, HEAD_DIM), F32)], axis=0)
    return cos, sin


def _relayout_w_in(w_in):
    d = w_in.shape[0]
    o_dt = COL_GLU
    main = w_in[:, :o_dt]
    dt = w_in[:, o_dt:o_dt + 2 * SSD_HEADS]
    glu = w_in[:, o_dt + 2 * SSD_HEADS:]
    zpad = jnp.zeros((d, LANES - SSD_HEADS), w_in.dtype)
    return jnp.concatenate([main, glu, dt[:, :SSD_HEADS], zpad, dt[:, SSD_HEADS:], zpad], axis=1).astype(BF16)


def kernel(x, c, ctx, c_ctx, w_mod, b_mod, norm_g, w_ffn_in, w_ffn_out, w_in, w_out, qk_g,
           ssd_conv_w, ssd_conv_b, ssd_dt_bias, ssd_a_log, ssd_d, ssd_norm_g,
           cm_dw_w, cm_dw_b, cm_ln_g, cm_ln_b, final_g):
    assert x.shape[0] == 1 and ctx.shape[0] == 1
    n_lat, d = x.shape[1], x.shape[2]
    n_ctx = ctx.shape[1]
    n = n_lat + n_ctx
    depth = w_mod.shape[0]
    assert n % MM_TILE == 0 and n_lat % ROW_TILE == 0 and n_ctx % ROW_TILE == 0 and n % KV_TILE == 0
    lat_tm = 512
    assert n_lat % lat_tm == 0 and n_lat % GRID_W == 0

    xs = jnp.concatenate([x[0], ctx[0]], axis=0)
    cvec = jnp.concatenate([c_ctx[None, :], c, jnp.zeros((SUBLANES - 2, d), F32)], axis=0)
    mods = _modulation(cvec, w_mod, b_mod)
    mods = mods.reshape(depth, SUBLANES, N_MOD, d).transpose(0, 2, 1, 3)
    cos, sin = _rope_tables(n_lat, n)
    w_ffn_in_b = w_ffn_in.astype(BF16)
    w_ffn_out_b = w_ffn_out.astype(BF16)
    w_out_b = w_out.astype(BF16)

    out = None
    for l in range(depth):
        last = l == depth - 1
        m = mods[l]
        xs = _ffn(xs, m[0:3], norm_g[l, 0], w_ffn_in_b[l, 0], w_ffn_out_b[l, 0],
                  rows=n, tm=MM_TILE, n_lat=n_lat)
        p = _in_proj(xs, m[3:6], norm_g[l, 1], _relayout_w_in(w_in[l]), tm=MM_TILE, n_lat=n_lat)
        q, k, v = _qkv_prep(p, cos, sin, qk_g[l])
        attn = _attention(q, k, v, q_tiles=n_lat // Q_TILE, q_off=0, kv_tiles=n // KV_TILE, kv_off=0,
                          tq=Q_TILE, tk=KV_TILE)
        if not last:
            attn = _attention(q, k, v, q_tiles=n_ctx // ROW_TILE, q_off=n_lat // ROW_TILE,
                              kv_tiles=n_ctx // ROW_TILE, kv_off=n_lat // ROW_TILE,
                              tq=ROW_TILE, tk=ROW_TILE, prev=attn)
        xa = _ssd_conv(p, ssd_conv_w[l], ssd_conv_b[l], n_lat=n_lat)
        y = _ssd_scan(xa, p, ssd_dt_bias[l], ssd_a_log[l], n_lat=n_lat)
        ssd = _ssd_out(y, xa, p, ssd_d[l], ssd_norm_g[l])
        cm = _conv_module(p, cm_dw_w[l], cm_dw_b[l], cm_ln_g[l], cm_ln_b[l], n_lat=n_lat)
        if last:
            xs = _out_proj(xs, attn, ssd, cm, m[3:6], w_out_b[l], rows=n_lat, tm=lat_tm // 2, n_lat=n_lat)
            out = _ffn(xs, m[6:9], norm_g[l, 2], w_ffn_in_b[l, 1], w_ffn_out_b[l, 1],
                       rows=n_lat, tm=lat_tm, n_lat=n_lat, final_g=final_g)
        else:
            xs = _out_proj(xs, attn, ssd, cm, m[3:6], w_out_b[l], rows=n, tm=MM_TILE // 2, n_lat=n_lat)
            xs = _ffn(xs, m[6:9], norm_g[l, 2], w_ffn_in_b[l, 1], w_ffn_out_b[l, 1],
                      rows=n, tm=MM_TILE, n_lat=n_lat)
    return out[None]
```

```python
import functools
import math

import jax
import jax.numpy as jnp
from jax import lax
from jax.experimental import pallas as pl
from jax.experimental.pallas import tpu as pltpu

F32 = jnp.float32
BF16 = jnp.bfloat16

D_MODEL = 2048
DEPTH = 2
GRID_W = 64
N_MOD = 9
D_FF = 5632
EPS = 1e-6

HEAD_DIM = 128
ATTN_HEADS = 8
ATTN_KV_HEADS = 2
ATTN_GROUPS = ATTN_HEADS // ATTN_KV_HEADS
ATTN_WIDTH = ATTN_HEADS * HEAD_DIM
KV_WIDTH = ATTN_KV_HEADS * HEAD_DIM
ROPE_THETA = 10000.0
ROPE_AXIS_DIM = HEAD_DIM // 2

SSD_HEADS = 8
SSD_HEAD_DIM = 64
SSD_WIDTH = SSD_HEADS * SSD_HEAD_DIM
SSD_GROUPS = 2
SSD_STATE = 128
SSD_CONV = 7
SSD_CHUNK = 128
SSD_CONV_CH = SSD_WIDTH + 2 * SSD_GROUPS * SSD_STATE

CM_WIDTH = 512
CM_KERNEL = 31
D_MIX = ATTN_WIDTH + SSD_WIDTH + CM_WIDTH

LANES = 128
SUBLANES = 8
VMEM_LIMIT = 56 * 1024 * 1024

COL_Q = 0
COL_K = COL_Q + ATTN_WIDTH
COL_V = COL_K + KV_WIDTH
COL_Z = COL_V + KV_WIDTH
COL_XBC = COL_Z + SSD_WIDTH
COL_GLU = COL_XBC + SSD_CONV_CH
COL_DT = COL_GLU + 2 * CM_WIDTH
P_COLS = COL_DT + 2 * LANES

ROW_TILE = 256
MM_TILE = 640
FF_CHUNK = 256
KV_TILE = 3328
Q_TILE = 256
SSD_HALO = 8
CM_HALO = 16
BF16_ROWS = 16
VT_ROWS = HEAD_DIM + BF16_ROWS
QK_EXP2_SCALE = (HEAD_DIM ** -0.5) * math.log2(math.e)
KEY_CHUNK = 256
SCORE_BUFFERS = 4


def _cparams(sem):
    return pltpu.CompilerParams(dimension_semantics=sem, vmem_limit_bytes=VMEM_LIMIT)


def _sigmoid(x):
    return 1.0 / (1.0 + jnp.exp(-x))


def _silu(x):
    return x * _sigmoid(x)


def _rms(x, g):
    return x * lax.rsqrt(jnp.mean(x * x, axis=-1, keepdims=True) + EPS) * g


def _mod_rows(mod_ref, k, row0, rows, n_lat):
    ridx = row0 + lax.broadcasted_iota(jnp.int32, (rows, 1), 0)
    return jnp.where(ridx < n_lat, mod_ref[k, 1:2, :], mod_ref[k, 0:1, :])


ROW_CHUNK = 128


def _norm_mod_store(h_scr, x_ref, mod_ref, g_ref, row0, tm, n_lat):
    for r in range(0, tm, ROW_CHUNK):
        shift = _mod_rows(mod_ref, 0, row0 + r, ROW_CHUNK, n_lat)
        scale = _mod_rows(mod_ref, 1, row0 + r, ROW_CHUNK, n_lat)
        h = _rms(x_ref[r:r + ROW_CHUNK, :], g_ref[...]) * (1.0 + scale) + shift
        h_scr[r:r + ROW_CHUNK, :] = h.astype(BF16)


def _mod_kernel(c_ref, w_ref, b_ref, o_ref):
    a = _silu(c_ref[...]).astype(BF16)
    o_ref[0] = jnp.dot(a, w_ref[0].astype(BF16), preferred_element_type=F32) + b_ref[0]


def _modulation(cvec, w_mod, b_mod):
    depth, d, n = w_mod.shape
    tn = 1024
    return pl.pallas_call(
        _mod_kernel,
        grid=(depth, n // tn),
        in_specs=[
            pl.BlockSpec((SUBLANES, d), lambda l, j: (0, 0)),
            pl.BlockSpec((1, d, tn), lambda l, j: (l, 0, j)),
            pl.BlockSpec((1, 1, tn), lambda l, j: (l, 0, j)),
        ],
        out_specs=pl.BlockSpec((1, SUBLANES, tn), lambda l, j: (l, 0, j)),
        out_shape=jax.ShapeDtypeStruct((depth, SUBLANES, n), F32),
        compiler_params=_cparams(("arbitrary", "arbitrary")),
        name="modulation",
    )(cvec, w_mod, b_mod.reshape(depth, 1, n))


def _ffn_kernel(x_ref, mod_ref, g_ref, wg_ref, wu_ref, wo_ref, *rest, n_lat, tm, final):
    if final:
        fg_ref, o_ref, h_scr, acc_scr = rest
    else:
        o_ref, h_scr, acc_scr = rest
    i = pl.program_id(0)
    f = pl.program_id(1)

    @pl.when(f == 0)
    def _():
        _norm_mod_store(h_scr, x_ref, mod_ref, g_ref, i * tm, tm, n_lat)
        acc_scr[...] = jnp.zeros_like(acc_scr)

    h = h_scr[...]
    gt = jnp.dot(h, wg_ref[...], preferred_element_type=F32)
    up = jnp.dot(h, wu_ref[...], preferred_element_type=F32)
    a = (_silu(gt) * up).astype(BF16)
    acc_scr[...] += jnp.dot(a, wo_ref[...], preferred_element_type=F32)

    @pl.when(f == pl.num_programs(1) - 1)
    def _():
        for r in range(0, tm, ROW_CHUNK):
            rs = slice(r, r + ROW_CHUNK)
            gate = _mod_rows(mod_ref, 2, i * tm + r, ROW_CHUNK, n_lat)
            y = x_ref[rs, :] + (0.5 * gate) * acc_scr[rs, :]
            if final:
                y = _rms(y, fg_ref[...])
            o_ref[rs, :] = y


def _ffn(xs, mod3, norm_g, w_in, w_out, *, rows, tm, n_lat, final_g=None):
    d = xs.shape[1]
    nf = D_FF // FF_CHUNK
    final = final_g is not None
    in_specs = [
        pl.BlockSpec((tm, d), lambda i, f: (i, 0)),
        pl.BlockSpec((3, SUBLANES, d), lambda i, f: (0, 0, 0)),
        pl.BlockSpec((1, d), lambda i, f: (0, 0)),
        pl.BlockSpec((d, FF_CHUNK), lambda i, f: (0, f)),
        pl.BlockSpec((d, FF_CHUNK), lambda i, f: (0, nf + f)),
        pl.BlockSpec((FF_CHUNK, d), lambda i, f: (f, 0)),
    ]
    args = [xs, mod3, norm_g.reshape(1, d), w_in, w_in, w_out]
    if final:
        in_specs.append(pl.BlockSpec((1, d), lambda i, f: (0, 0)))
        args.append(final_g.reshape(1, d))
    return pl.pallas_call(
        functools.partial(_ffn_kernel, n_lat=n_lat, tm=tm, final=final),
        grid=(rows // tm, nf),
        in_specs=in_specs,
        out_specs=pl.BlockSpec((tm, d), lambda i, f: (i, 0)),
        out_shape=jax.ShapeDtypeStruct((rows, d), F32),
        scratch_shapes=[pltpu.VMEM((tm, d), BF16), pltpu.VMEM((tm, d), F32)],
        compiler_params=_cparams(("parallel", "arbitrary")),
        name="ffn",
    )(*args)


def _proj_kernel(x_ref, mod_ref, g_ref, w_ref, o_ref, h_scr, *, n_lat, tm):
    i = pl.program_id(0)

    @pl.when(pl.program_id(1) == 0)
    def _():
        _norm_mod_store(h_scr, x_ref, mod_ref, g_ref, i * tm, tm, n_lat)

    o_ref[...] = jnp.dot(h_scr[...], w_ref[...], preferred_element_type=F32)


def _in_proj(xs, mod3, norm_g, w_in_r, *, tm, n_lat):
    n, d = xs.shape
    tn = P_COLS // 2
    return pl.pallas_call(
        functools.partial(_proj_kernel, n_lat=n_lat, tm=tm),
        grid=(n // tm, P_COLS // tn),
        in_specs=[
            pl.BlockSpec((tm, d), lambda i, j: (i, 0)),
            pl.BlockSpec((3, SUBLANES, d), lambda i, j: (0, 0, 0)),
            pl.BlockSpec((1, d), lambda i, j: (0, 0)),
            pl.BlockSpec((d, tn), lambda i, j: (0, j)),
        ],
        out_specs=pl.BlockSpec((tm, tn), lambda i, j: (i, j)),
        out_shape=jax.ShapeDtypeStruct((n, P_COLS), F32),
        scratch_shapes=[pltpu.VMEM((tm, d), BF16)],
        compiler_params=_cparams(("parallel", "arbitrary")),
        name="in_proj",
    )(xs, mod3, norm_g.reshape(1, d), w_in_r)


def _qkv_kernel(q_ref, kv_ref, cos_ref, sin_ref, g_ref, qo_ref, ko_ref, vo_ref):
    cos = cos_ref[...]
    sin = sin_ref[...]
    lane = lax.broadcasted_iota(jnp.int32, cos.shape, 1)
    first_half = (lane % ROPE_AXIS_DIM) < (ROPE_AXIS_DIM // 2)

    def norm_rope(xh, g):
        xh = _rms(xh, g)
        swapped = jnp.where(first_half, pltpu.roll(xh, LANES - 32, 1), pltpu.roll(xh, 32, 1))
        return xh * cos + swapped * sin

    for h in range(ATTN_HEADS):
        sl = slice(h * HEAD_DIM, (h + 1) * HEAD_DIM)
        qo_ref[:, sl] = (norm_rope(q_ref[:, sl], g_ref[0:1, :]) * QK_EXP2_SCALE).astype(BF16)
    for h in range(ATTN_KV_HEADS):
        sl = slice(h * HEAD_DIM, (h + 1) * HEAD_DIM)
        ko_ref[:, sl] = norm_rope(kv_ref[:, sl], g_ref[1:2, :]).astype(BF16)
    vt = kv_ref[:, KV_WIDTH:].T.astype(BF16)
    ones = jnp.ones((VT_ROWS - HEAD_DIM, vt.shape[1]), BF16)
    for g in range(ATTN_KV_HEADS):
        vo_ref[g * VT_ROWS:g * VT_ROWS + HEAD_DIM, :] = vt[g * HEAD_DIM:(g + 1) * HEAD_DIM, :]
        vo_ref[g * VT_ROWS + HEAD_DIM:(g + 1) * VT_ROWS, :] = ones


def _qkv_prep(p, cos, sin, qk_g):
    n = p.shape[0]
    t = ROW_TILE
    return pl.pallas_call(
        _qkv_kernel,
        grid=(n // t,),
        in_specs=[
            pl.BlockSpec((t, ATTN_WIDTH), lambda i: (i, COL_Q // ATTN_WIDTH)),
            pl.BlockSpec((t, 2 * KV_WIDTH), lambda i: (i, COL_K // (2 * KV_WIDTH))),
            pl.BlockSpec((t, HEAD_DIM), lambda i: (i, 0)),
            pl.BlockSpec((t, HEAD_DIM), lambda i: (i, 0)),
            pl.BlockSpec((2, HEAD_DIM), lambda i: (0, 0)),
        ],
        out_specs=[
            pl.BlockSpec((t, ATTN_WIDTH), lambda i: (i, 0)),
            pl.BlockSpec((t, KV_WIDTH), lambda i: (i, 0)),
            pl.BlockSpec((ATTN_KV_HEADS * VT_ROWS, t), lambda i: (0, i)),
        ],
        out_shape=[
            jax.ShapeDtypeStruct((n, ATTN_WIDTH), BF16),
            jax.ShapeDtypeStruct((n, KV_WIDTH), BF16),
            jax.ShapeDtypeStruct((ATTN_KV_HEADS * VT_ROWS, n), BF16),
        ],
        compiler_params=_cparams(("parallel",)),
        name="qkv_prep",
    )(p, p, cos, sin, qk_g)


def _flash_kernel(*refs, aliased):
    if aliased:
        q_ref, k_ref, vt_ref, _, o_ref, m_scr, acc_scr, *bufs = refs
    else:
        q_ref, k_ref, vt_ref, o_ref, m_scr, acc_scr, *bufs = refs
    s_bufs, p_bufs = bufs[:len(bufs) // 2], bufs[len(bufs) // 2:]
    j = pl.program_id(1)
    tq = q_ref.shape[0]
    tk = k_ref.shape[0]
    w = ATTN_GROUPS * tq
    kc = min(KEY_CHUNK, tk)

    @pl.when(j == 0)
    def _():
        m_scr[...] = jnp.full_like(m_scr, -jnp.inf)
        acc_scr[...] = jnp.zeros_like(acc_scr)

    qs = [jnp.concatenate(
        [q_ref[:, (ATTN_GROUPS * g + h) * HEAD_DIM:(ATTN_GROUPS * g + h + 1) * HEAD_DIM]
         for h in range(ATTN_GROUPS)], axis=0) for g in range(ATTN_KV_HEADS)]
    units = [(g, ci) for ci in range(tk // kc) for g in range(ATTN_KV_HEADS)]

    def scores(u):
        g, ci = units[u]
        st = lax.dot_general(k_ref[ci * kc:(ci + 1) * kc, g * HEAD_DIM:(g + 1) * HEAD_DIM], qs[g],
                             (((1,), (1,)), ((), ())), preferred_element_type=F32)
        s_bufs[u % len(s_bufs)][...] = st
        mx = st[0:SUBLANES]
        for r in range(1, kc // SUBLANES):
            mx = jnp.maximum(mx, st[r * SUBLANES:(r + 1) * SUBLANES])
        return mx

    m_run = [m_scr[g] for g in range(ATTN_KV_HEADS)]
    mx_next = scores(0)
    for u, (g, ci) in enumerate(units):
        mx = mx_next
        if u + 1 < len(units):
            mx_next = scores(u + 1)
        m_new = jnp.maximum(m_run[g], jnp.max(mx, axis=0, keepdims=True))
        alpha = jnp.exp2(m_run[g] - m_new)
        m_rep = jnp.broadcast_to(m_new, (BF16_ROWS, w))
        s_buf, p_buf = s_bufs[u % len(s_bufs)], p_bufs[u % len(p_bufs)]
        for r in range(kc // BF16_ROWS):
            rs = slice(r * BF16_ROWS, (r + 1) * BF16_ROWS)
            p_buf[rs, :] = jnp.exp2(s_buf[rs, :] - m_rep).astype(BF16)
        pv = jnp.dot(vt_ref[g * VT_ROWS:(g + 1) * VT_ROWS, ci * kc:(ci + 1) * kc], p_buf[...],
                     preferred_element_type=F32)
        acc_scr[g] = alpha * acc_scr[g] + pv
        m_run[g] = m_new
    for g in range(ATTN_KV_HEADS):
        m_scr[g] = m_run[g]

    @pl.when(j == pl.num_programs(1) - 1)
    def _():
        for g in range(ATTN_KV_HEADS):
            acc = acc_scr[g]
            ot = acc[0:HEAD_DIM] / acc[HEAD_DIM:HEAD_DIM + 1]
            for h in range(ATTN_GROUPS):
                col = (ATTN_GROUPS * g + h) * HEAD_DIM
                o_ref[:, col:col + HEAD_DIM] = ot[:, h * tq:(h + 1) * tq].T.astype(o_ref.dtype)


def _attention(q, k, vt, *, q_tiles, q_off, kv_tiles, kv_off, tq, tk, prev=None):
    n = q.shape[0]
    aliased = prev is not None
    in_specs = [
        pl.BlockSpec((tq, ATTN_WIDTH), lambda i, j: (q_off + i, 0)),
        pl.BlockSpec((tk, KV_WIDTH), lambda i, j: (kv_off + j, 0)),
        pl.BlockSpec((ATTN_KV_HEADS * VT_ROWS, tk), lambda i, j: (0, kv_off + j)),
    ]
    args = [q, k, vt]
    if aliased:
        in_specs.append(pl.BlockSpec(memory_space=pl.ANY))
        args.append(prev)
    w = ATTN_GROUPS * tq
    kc = min(KEY_CHUNK, tk)
    units = min(ATTN_KV_HEADS * (tk // kc), SCORE_BUFFERS)
    return pl.pallas_call(
        functools.partial(_flash_kernel, aliased=aliased),
        grid=(q_tiles, kv_tiles),
        in_specs=in_specs,
        out_specs=pl.BlockSpec((tq, ATTN_WIDTH), lambda i, j: (q_off + i, 0)),
        out_shape=jax.ShapeDtypeStruct((n, ATTN_WIDTH), BF16),
        scratch_shapes=[
            pltpu.VMEM((ATTN_KV_HEADS, 1, w), F32),
            pltpu.VMEM((ATTN_KV_HEADS, VT_ROWS, w), F32),
        ] + [pltpu.VMEM((kc, w), F32)] * units + [pltpu.VMEM((kc, w), BF16)] * units,
        input_output_aliases={3: 0} if aliased else {},
        compiler_params=_cparams(("parallel", "arbitrary")),
        name="attention_ctx" if aliased else "attention",
    )(*args)


def _segment_edges(i, lat_tiles):
    first = jnp.logical_or(i == 0, i == lat_tiles)
    last = jnp.logical_or(i == lat_tiles - 1, i == pl.num_programs(0) - 1)
    return first, last


def _ssd_conv_kernel(prev_ref, cur_ref, next_ref, w_ref, b_ref, o_ref, ext_scr, *, lat_tiles):
    i = pl.program_id(0)
    t = cur_ref.shape[0]
    first, last = _segment_edges(i, lat_tiles)
    ext_scr[0:SSD_HALO, :] = jnp.where(first, 0.0, prev_ref[...])
    ext_scr[SSD_HALO:SSD_HALO + t, :] = cur_ref[...]
    ext_scr[SSD_HALO + t:, :] = jnp.where(last, 0.0, next_ref[...])
    pad = SSD_CONV // 2
    acc = jnp.broadcast_to(b_ref[...], cur_ref.shape)
    for k in range(SSD_CONV):
        acc = acc + w_ref[k:k + 1, :] * ext_scr[pl.ds(SSD_HALO - pad + k, t), :]
    o_ref[...] = _silu(acc)


def _ssd_conv(p, conv_w, conv_b, *, n_lat):
    n = p.shape[0]
    t = ROW_TILE
    hb = t // SSD_HALO
    last_hb = n // SSD_HALO - 1
    cb = COL_XBC // SSD_CONV_CH
    return pl.pallas_call(
        functools.partial(_ssd_conv_kernel, lat_tiles=n_lat // t),
        grid=(n // t,),
        in_specs=[
            pl.BlockSpec((SSD_HALO, SSD_CONV_CH), lambda i: (jnp.maximum(i * hb - 1, 0), cb)),
            pl.BlockSpec((t, SSD_CONV_CH), lambda i: (i, cb)),
            pl.BlockSpec((SSD_HALO, SSD_CONV_CH), lambda i: (jnp.minimum((i + 1) * hb, last_hb), cb)),
            pl.BlockSpec((SSD_CONV, SSD_CONV_CH), lambda i: (0, 0)),
            pl.BlockSpec((1, SSD_CONV_CH), lambda i: (0, 0)),
        ],
        out_specs=pl.BlockSpec((t, SSD_CONV_CH), lambda i: (i, 0)),
        out_shape=jax.ShapeDtypeStruct((n, SSD_CONV_CH), F32),
        scratch_shapes=[pltpu.VMEM((t + 2 * SSD_HALO, SSD_CONV_CH), F32)],
        compiler_params=_cparams(("parallel",)),
        name="ssd_conv",
    )(p, p, p, conv_w, conv_b.reshape(1, SSD_CONV_CH))


def _cm_kernel(ap_ref, gp_ref, a_ref, g_ref, an_ref, gn_ref, w_ref, b_ref, lg_ref, lb_ref, o_ref, ext_scr,
               *, lat_tiles):
    i = pl.program_id(0)
    t = a_ref.shape[0]
    first, last = _segment_edges(i, lat_tiles)

    def glu(a, gt):
        return a[...] * _sigmoid(gt[...])

    ext_scr[0:CM_HALO, :] = jnp.where(first, 0.0, glu(ap_ref, gp_ref))
    ext_scr[CM_HALO:CM_HALO + t, :] = glu(a_ref, g_ref)
    ext_scr[CM_HALO + t:, :] = jnp.where(last, 0.0, glu(an_ref, gn_ref))
    pad = CM_KERNEL // 2
    acc = jnp.broadcast_to(b_ref[...], a_ref.shape)
    for k in range(CM_KERNEL):
        acc = acc + w_ref[k:k + 1, :] * ext_scr[pl.ds(CM_HALO - pad + k, t), :]
    mu = jnp.mean(acc, axis=-1, keepdims=True)
    cen = acc - mu
    var = jnp.mean(cen * cen, axis=-1, keepdims=True)
    y = cen * lax.rsqrt(var + EPS) * lg_ref[...] + lb_ref[...]
    o_ref[...] = _silu(y).astype(o_ref.dtype)


def _conv_module(p, dw_w, dw_b, ln_g, ln_b, *, n_lat):
    n = p.shape[0]
    t = ROW_TILE
    hb = t // CM_HALO
    last_hb = n // CM_HALO - 1
    ca = COL_GLU // CM_WIDTH
    cg = ca + 1

    def prev_map(c):
        return lambda i: (jnp.maximum(i * hb - 1, 0), c)

    def next_map(c):
        return lambda i: (jnp.minimum((i + 1) * hb, last_hb), c)

    vec = pl.BlockSpec((1, CM_WIDTH), lambda i: (0, 0))
    return pl.pallas_call(
        functools.partial(_cm_kernel, lat_tiles=n_lat // t),
        grid=(n // t,),
        in_specs=[
            pl.BlockSpec((CM_HALO, CM_WIDTH), prev_map(ca)),
            pl.BlockSpec((CM_HALO, CM_WIDTH), prev_map(cg)),
            pl.BlockSpec((t, CM_WIDTH), lambda i: (i, ca)),
            pl.BlockSpec((t, CM_WIDTH), lambda i: (i, cg)),
            pl.BlockSpec((CM_HALO, CM_WIDTH), next_map(ca)),
            pl.BlockSpec((CM_HALO, CM_WIDTH), next_map(cg)),
            pl.BlockSpec((CM_KERNEL, CM_WIDTH), lambda i: (0, 0)),
            vec, vec, vec,
        ],
        out_specs=pl.BlockSpec((t, CM_WIDTH), lambda i: (i, 0)),
        out_shape=jax.ShapeDtypeStruct((n, CM_WIDTH), BF16),
        scratch_shapes=[pltpu.VMEM((t + 2 * CM_HALO, CM_WIDTH), F32)],
        compiler_params=_cparams(("parallel",)),
        name="conv_module",
    )(p, p, p, p, p, p, dw_w, dw_b.reshape(1, -1), ln_g.reshape(1, -1), ln_b.reshape(1, -1))


def _split3_dot(tri, x):
    hi = x.astype(BF16)
    r1 = x - hi.astype(F32)
    mid = r1.astype(BF16)
    lo = (r1 - mid.astype(F32)).astype(BF16)
    return (jnp.dot(tri, hi, preferred_element_type=F32) + jnp.dot(tri, mid, preferred_element_type=F32)
            + jnp.dot(tri, lo, preferred_element_type=F32))


def _ssd_scan_kernel(xa_ref, dt_ref, dtb_ref, alog_ref, y_ref, h_scr):
    d = pl.program_id(0)
    q = SSD_CHUNK

    @pl.when(pl.program_id(1) == 0)
    def _():
        h_scr[...] = jnp.zeros_like(h_scr)

    x = dt_ref[...] + dtb_ref[0]
    dt = jnp.maximum(x, 0.0) + jnp.log1p(jnp.exp(-jnp.abs(x)))
    da = dt * (-jnp.exp(alog_ref[0]))
    ii = lax.broadcasted_iota(jnp.int32, (q, q), 0)
    jj = lax.broadcasted_iota(jnp.int32, (q, q), 1)
    lane = jj
    inc = ((jj - ii) * (1 - 2 * d)) <= 0
    tri = jnp.where(inc, 1.0, 0.0).astype(BF16)
    cum = _split3_dot(tri, da)
    tot = jnp.sum(da, axis=0, keepdims=True)
    cum_t = cum.T
    dt_t = dt.T
    e_in = jnp.exp(cum)
    w_out = jnp.exp(tot - cum) * dt
    e_tot = jnp.exp(tot)

    xa = xa_ref[...]
    pair = 2 * SSD_HEAD_DIM
    low = lane < SSD_HEAD_DIM
    for g in range(SSD_GROUPS):
        b = xa[:, SSD_WIDTH + g * SSD_STATE:SSD_WIDTH + (g + 1) * SSD_STATE].astype(BF16)
        cg = xa[:, SSD_WIDTH + (SSD_GROUPS + g) * SSD_STATE:
                SSD_WIDTH + (SSD_GROUPS + g + 1) * SSD_STATE].astype(BF16)
        cb = lax.dot_general(cg, b, (((1,), (1,)), ((), ())), preferred_element_type=F32)
        for m in range(SSD_HEADS // SSD_GROUPS // 2):
            pi = g * (SSD_HEADS // SSD_GROUPS // 2) + m
            h0, h1 = 2 * pi, 2 * pi + 1
            xp = xa[:, pi * pair:(pi + 1) * pair]
            xp_b = xp.astype(BF16)
            ys = []
            for hd in (h0, h1):
                seg = cum[:, hd:hd + 1] - cum_t[hd:hd + 1, :]
                dec = jnp.exp(jnp.where(inc, seg, -jnp.inf))
                sc = (cb * dec * dt_t[hd:hd + 1, :]).astype(BF16)
                ys.append(jnp.dot(sc, xp_b, preferred_element_type=F32))
            y_diag = jnp.where(low, ys[0], ys[1])
            hp = h_scr[pi]
            y_off = lax.dot_general(cg, hp.astype(BF16), (((1,), (1,)), ((), ())),
                                    preferred_element_type=F32)
            y_off = y_off * jnp.where(low, e_in[:, h0:h0 + 1], e_in[:, h1:h1 + 1])
            y_ref[0, :, pi * pair:(pi + 1) * pair] = y_diag + y_off
            xw = xp * jnp.where(low, w_out[:, h0:h0 + 1], w_out[:, h1:h1 + 1])
            st = jnp.dot(xw.T.astype(BF16), b, preferred_element_type=F32)
            row_low = ii < SSD_HEAD_DIM
            keep = jnp.where(row_low, e_tot[:, h0:h0 + 1], e_tot[:, h1:h1 + 1])
            h_scr[pi] = keep * hp + st


def _ssd_scan(xa, p, dt_bias, a_log, *, n_lat):
    n = xa.shape[0]
    nc = n // SSD_CHUNK
    lat_c = n_lat // SSD_CHUNK
    ctx_c = nc - lat_c
    dt_cb = COL_DT // LANES

    def chunk(d, s):
        fwd = jnp.where(s < ctx_c, lat_c + s, s - ctx_c)
        bwd = jnp.where(s < ctx_c, nc - 1 - s, nc - 1 - s)
        return jnp.where(d == 0, fwd, bwd)

    pad = LANES - SSD_HEADS
    dtb = jnp.pad(dt_bias, ((0, 0), (0, pad))).reshape(2, 1, LANES)
    alog = jnp.pad(a_log, ((0, 0), (0, pad))).reshape(2, 1, LANES)
    return pl.pallas_call(
        _ssd_scan_kernel,
        grid=(2, nc),
        in_specs=[
            pl.BlockSpec((SSD_CHUNK, SSD_CONV_CH), lambda d, s: (chunk(d, s), 0)),
            pl.BlockSpec((SSD_CHUNK, LANES), lambda d, s: (chunk(d, s), dt_cb + d)),
            pl.BlockSpec((1, 1, LANES), lambda d, s: (d, 0, 0)),
            pl.BlockSpec((1, 1, LANES), lambda d, s: (d, 0, 0)),
        ],
        out_specs=pl.BlockSpec((1, SSD_CHUNK, SSD_WIDTH), lambda d, s: (d, chunk(d, s), 0)),
        out_shape=jax.ShapeDtypeStruct((2, n, SSD_WIDTH), F32),
        scratch_shapes=[pltpu.VMEM((SSD_HEADS // 2, 2 * SSD_HEAD_DIM, SSD_STATE), F32)],
        compiler_params=_cparams(("arbitrary", "arbitrary")),
        name="ssd_scan",
    )(xa, p, dtb, alog)


def _ssd_out_kernel(y_ref, xs_ref, z_ref, d_ref, g_ref, o_ref):
    y = y_ref[0] + y_ref[1] + d_ref[...] * xs_ref[...]
    y = y * _silu(z_ref[...])
    o_ref[...] = _rms(y, g_ref[...]).astype(o_ref.dtype)


def _ssd_out(y, xa, p, d_skip, norm_g):
    n = xa.shape[0]
    t = ROW_TILE
    vec = pl.BlockSpec((1, SSD_WIDTH), lambda i: (0, 0))
    return pl.pallas_call(
        _ssd_out_kernel,
        grid=(n // t,),
        in_specs=[
            pl.BlockSpec((2, t, SSD_WIDTH), lambda i: (0, i, 0)),
            pl.BlockSpec((t, SSD_WIDTH), lambda i: (i, 0)),
            pl.BlockSpec((t, SSD_WIDTH), lambda i: (i, COL_Z // SSD_WIDTH)),
            vec, vec,
        ],
        out_specs=pl.BlockSpec((t, SSD_WIDTH), lambda i: (i, 0)),
        out_shape=jax.ShapeDtypeStruct((n, SSD_WIDTH), BF16),
        compiler_params=_cparams(("parallel",)),
        name="ssd_out",
    )(y, xa, p, jnp.repeat(d_skip, SSD_HEAD_DIM).reshape(1, SSD_WIDTH), norm_g.reshape(1, SSD_WIDTH))


def _out_proj_kernel(x_ref, a_ref, s_ref, c_ref, mod_ref, w_ref, o_ref, *, n_lat, tm):
    i = pl.program_id(0)
    acc = jnp.dot(a_ref[...], w_ref[0:ATTN_WIDTH, :], preferred_element_type=F32)
    acc += jnp.dot(s_ref[...], w_ref[ATTN_WIDTH:ATTN_WIDTH + SSD_WIDTH, :], preferred_element_type=F32)
    acc += jnp.dot(c_ref[...], w_ref[ATTN_WIDTH + SSD_WIDTH:, :], preferred_element_type=F32)
    gate = _mod_rows(mod_ref, 2, i * tm, tm, n_lat)
    o_ref[...] = x_ref[...] + gate * acc


def _out_proj(xs, attn, ssd, cm, mod3, w_out, *, rows, tm, n_lat):
    d = xs.shape[1]
    return pl.pallas_call(
        functools.partial(_out_proj_kernel, n_lat=n_lat, tm=tm),
        grid=(rows // tm,),
        in_specs=[
            pl.BlockSpec((tm, d), lambda i: (i, 0)),
            pl.BlockSpec((tm, ATTN_WIDTH), lambda i: (i, 0)),
            pl.BlockSpec((tm, SSD_WIDTH), lambda i: (i, 0)),
            pl.BlockSpec((tm, CM_WIDTH), lambda i: (i, 0)),
            pl.BlockSpec((3, SUBLANES, d), lambda i: (0, 0, 0)),
            pl.BlockSpec((D_MIX, d), lambda i: (0, 0)),
        ],
        out_specs=pl.BlockSpec((tm, d), lambda i: (i, 0)),
        out_shape=jax.ShapeDtypeStruct((rows, d), F32),
        compiler_params=_cparams(("parallel",)),
        name="out_proj",
    )(xs, attn, ssd, cm, mod3, w_out)


def _rope_tables(n_lat, n):
    rows = n_lat // GRID_W
    row = jnp.repeat(jnp.arange(rows), GRID_W).astype(F32)
    col = jnp.tile(jnp.arange(GRID_W), rows).astype(F32)
    inv = ROPE_THETA ** (-jnp.arange(0, ROPE_AXIS_DIM, 2, dtype=F32) / ROPE_AXIS_DIM)
    ar = row[:, None] * inv
    ac = col[:, None] * inv
    cos = jnp.concatenate([jnp.cos(ar), jnp.cos(ar), jnp.cos(ac), jnp.cos(ac)], axis=1)
    sin = jnp.concatenate([-jnp.sin(ar), jnp.sin(ar), -jnp.sin(ac), jnp.sin(ac)], axis=1)
    pad = n - n_lat
    cos = jnp.concatenate([cos, jnp.ones((pad, HEAD_DIM), F32)], axis=0)
    sin = jnp.concatenate([sin, jnp.zeros((pad, HEAD_DIM), F32)], axis=0)
    return cos, sin


def _relayout_w_in(w_in):
    d = w_in.shape[0]
    o_dt = COL_GLU
    main = w_in[:, :o_dt]
    dt = w_in[:, o_dt:o_dt + 2 * SSD_HEADS]
    glu = w_in[:, o_dt + 2 * SSD_HEADS:]
    zpad = jnp.zeros((d, LANES - SSD_HEADS), w_in.dtype)
    return jnp.concatenate([main, glu, dt[:, :SSD_HEADS], zpad, dt[:, SSD_HEADS:], zpad], axis=1).astype(BF16)


def kernel(x, c, ctx, c_ctx, w_mod, b_mod, norm_g, w_ffn_in, w_ffn_out, w_in, w_out, qk_g,
           ssd_conv_w, ssd_conv_b, ssd_dt_bias, ssd_a_log, ssd_d, ssd_norm_g,
           cm_dw_w, cm_dw_b, cm_ln_g, cm_ln_b, final_g):
    assert x.shape[0] == 1 and ctx.shape[0] == 1
    n_lat, d = x.shape[1], x.shape[2]
    n_ctx = ctx.shape[1]
    n = n_lat + n_ctx
    depth = w_mod.shape[0]
    assert n % MM_TILE == 0 and n_lat % ROW_TILE == 0 and n_ctx % ROW_TILE == 0
    kv_tile = max(t for t in range(KEY_CHUNK, KV_TILE + 1, KEY_CHUNK) if n % t == 0)
    lat_tm = 512
    assert n_lat % lat_tm == 0 and n_lat % GRID_W == 0

    xs = jnp.concatenate([x[0], ctx[0]], axis=0)
    cvec = jnp.concatenate([c_ctx[None, :], c, jnp.zeros((SUBLANES - 2, d), F32)], axis=0)
    mods = _modulation(cvec, w_mod, b_mod)
    mods = mods.reshape(depth, SUBLANES, N_MOD, d).transpose(0, 2, 1, 3)
    cos, sin = _rope_tables(n_lat, n)
    w_ffn_in_b = w_ffn_in.astype(BF16)
    w_ffn_out_b = w_ffn_out.astype(BF16)
    w_out_b = w_out.astype(BF16)

    out = None
    for l in range(depth):
        last = l == depth - 1
        m = mods[l]
        xs = _ffn(xs, m[0:3], norm_g[l, 0], w_ffn_in_b[l, 0], w_ffn_out_b[l, 0],
                  rows=n, tm=MM_TILE, n_lat=n_lat)
        p = _in_proj(xs, m[3:6], norm_g[l, 1], _relayout_w_in(w_in[l]), tm=MM_TILE, n_lat=n_lat)
        q, k, v = _qkv_prep(p, cos, sin, qk_g[l])
        attn = _attention(q, k, v, q_tiles=n_lat // Q_TILE, q_off=0, kv_tiles=n // kv_tile, kv_off=0,
                          tq=Q_TILE, tk=kv_tile)
        if not last:
            attn = _attention(q, k, v, q_tiles=n_ctx // ROW_TILE, q_off=n_lat // ROW_TILE,
                              kv_tiles=n_ctx // ROW_TILE, kv_off=n_lat // ROW_TILE,
                              tq=ROW_TILE, tk=ROW_TILE, prev=attn)
        xa = _ssd_conv(p, ssd_conv_w[l], ssd_conv_b[l], n_lat=n_lat)
        y = _ssd_scan(xa, p, ssd_dt_bias[l], ssd_a_log[l], n_lat=n_lat)
        ssd = _ssd_out(y, xa, p, ssd_d[l], ssd_norm_g[l])
        cm = _conv_module(p, cm_dw_w[l], cm_dw_b[l], cm_ln_g[l], cm_ln_b[l], n_lat=n_lat)
        if last:
            xs = _out_proj(xs, attn, ssd, cm, m[3:6], w_out_b[l], rows=n_lat, tm=lat_tm // 2, n_lat=n_lat)
            out = _ffn(xs, m[6:9], norm_g[l, 2], w_ffn_in_b[l, 1], w_ffn_out_b[l, 1],
                       rows=n_lat, tm=lat_tm, n_lat=n_lat, final_g=final_g)
        else:
            xs = _out_proj(xs, attn, ssd, cm, m[3:6], w_out_b[l], rows=n, tm=MM_TILE // 2, n_lat=n_lat)
            xs = _ffn(xs, m[6:9], norm_g[l, 2], w_ffn_in_b[l, 1], w_ffn_out_b[l, 1],
                      rows=n, tm=MM_TILE, n_lat=n_lat)
    return out[None]
```

```python
import functools
import math

import jax
import jax.numpy as jnp
from jax import lax
from jax.experimental import pallas as pl
from jax.experimental.pallas import tpu as pltpu

F32 = jnp.float32
BF16 = jnp.bfloat16

D_MODEL = 2048
DEPTH = 2
GRID_W = 64
N_MOD = 9
D_FF = 5632
EPS = 1e-6

HEAD_DIM = 128
ATTN_HEADS = 8
ATTN_KV_HEADS = 2
ATTN_GROUPS = ATTN_HEADS // ATTN_KV_HEADS
ATTN_WIDTH = ATTN_HEADS * HEAD_DIM
KV_WIDTH = ATTN_KV_HEADS * HEAD_DIM
ROPE_THETA = 10000.0
ROPE_AXIS_DIM = HEAD_DIM // 2

SSD_HEADS = 8
SSD_HEAD_DIM = 64
SSD_WIDTH = SSD_HEADS * SSD_HEAD_DIM
SSD_GROUPS = 2
SSD_STATE = 128
SSD_CONV = 7
SSD_CHUNK = 128
SSD_CONV_CH = SSD_WIDTH + 2 * SSD_GROUPS * SSD_STATE

CM_WIDTH = 512
CM_KERNEL = 31
D_MIX = ATTN_WIDTH + SSD_WIDTH + CM_WIDTH

LANES = 128
SUBLANES = 8
VMEM_LIMIT = 56 * 1024 * 1024

COL_Q = 0
COL_K = COL_Q + ATTN_WIDTH
COL_V = COL_K + KV_WIDTH
COL_Z = COL_V + KV_WIDTH
COL_XBC = COL_Z + SSD_WIDTH
COL_GLU = COL_XBC + SSD_CONV_CH
COL_DT = COL_GLU + 2 * CM_WIDTH
P_COLS = COL_DT + 2 * LANES

ROW_TILE = 256
MM_TILE = 640
FF_CHUNK = 512
KV_TILE = 3328
Q_TILE = 256
SSD_HALO = 8
CM_HALO = 16
BF16_ROWS = 16
VT_ROWS = HEAD_DIM + BF16_ROWS
QK_EXP2_SCALE = (HEAD_DIM ** -0.5) * math.log2(math.e)
KEY_CHUNK = 256
SCORE_BUFFERS = 4
SCORE_AHEAD = 2


def _cparams(sem):
    return pltpu.CompilerParams(dimension_semantics=sem, vmem_limit_bytes=VMEM_LIMIT)


def _sigmoid(x):
    return 1.0 / (1.0 + jnp.exp(-x))


def _silu(x):
    return x * _sigmoid(x)


def _rms(x, g):
    return x * lax.rsqrt(jnp.mean(x * x, axis=-1, keepdims=True) + EPS) * g


ROW_CHUNK = 64


def _mod_row(mod_ref, k, row0, n_lat):
    cls = (row0 < n_lat).astype(jnp.int32)
    return mod_ref[k, pl.ds(cls, 1), :]


def _norm_mod_store(h_scr, x_ref, mod_ref, g_ref, row0, tm, n_lat):
    for r in range(0, tm, ROW_CHUNK):
        shift = _mod_row(mod_ref, 0, row0 + r, n_lat)
        gain = g_ref[...] * (1.0 + _mod_row(mod_ref, 1, row0 + r, n_lat))
        x = x_ref[r:r + ROW_CHUNK, :]
        inv = lax.rsqrt(jnp.mean(x * x, axis=-1, keepdims=True) + EPS)
        h_scr[r:r + ROW_CHUNK, :] = ((x * inv) * gain + shift).astype(BF16)


def _mod_kernel(c_ref, w_ref, b_ref, o_ref):
    a = _silu(c_ref[...]).astype(BF16)
    o_ref[0] = jnp.dot(a, w_ref[0].astype(BF16), preferred_element_type=F32) + b_ref[0]


def _modulation(cvec, w_mod, b_mod):
    depth, d, n = w_mod.shape
    tn = 1024
    return pl.pallas_call(
        _mod_kernel,
        grid=(depth, n // tn),
        in_specs=[
            pl.BlockSpec((SUBLANES, d), lambda l, j: (0, 0)),
            pl.BlockSpec((1, d, tn), lambda l, j: (l, 0, j)),
            pl.BlockSpec((1, 1, tn), lambda l, j: (l, 0, j)),
        ],
        out_specs=pl.BlockSpec((1, SUBLANES, tn), lambda l, j: (l, 0, j)),
        out_shape=jax.ShapeDtypeStruct((depth, SUBLANES, n), F32),
        compiler_params=_cparams(("arbitrary", "arbitrary")),
        name="modulation",
    )(cvec, w_mod, b_mod.reshape(depth, 1, n))


def _ffn_kernel(x_ref, mod_ref, g_ref, wg_ref, wu_ref, wo_ref, *rest, n_lat, tm, final):
    if final:
        fg_ref, o_ref, h_scr, acc_scr = rest
    else:
        o_ref, h_scr, acc_scr = rest
    i = pl.program_id(0)
    f = pl.program_id(1)

    @pl.when(f == 0)
    def _():
        _norm_mod_store(h_scr, x_ref, mod_ref, g_ref, i * tm, tm, n_lat)
        acc_scr[...] = jnp.zeros_like(acc_scr)

    h = h_scr[...]
    gt = jnp.dot(h, wg_ref[...], preferred_element_type=F32)
    up = jnp.dot(h, wu_ref[...], preferred_element_type=F32)
    a = (_silu(gt) * up).astype(BF16)
    acc_scr[...] += jnp.dot(a, wo_ref[...], preferred_element_type=F32)

    @pl.when(f == pl.num_programs(1) - 1)
    def _():
        for r in range(0, tm, ROW_CHUNK):
            rs = slice(r, r + ROW_CHUNK)
            gate = _mod_row(mod_ref, 2, i * tm + r, n_lat)
            y = x_ref[rs, :] + (0.5 * gate) * acc_scr[rs, :]
            if final:
                y = _rms(y, fg_ref[...])
            o_ref[rs, :] = y


def _ffn(xs, mod3, norm_g, w_in, w_out, layer, which, *, rows, tm, n_lat, final_g=None):
    d = xs.shape[1]
    nf = D_FF // FF_CHUNK
    final = final_g is not None
    in_specs = [
        pl.BlockSpec((tm, d), lambda i, f: (i, 0)),
        pl.BlockSpec((3, SUBLANES, d), lambda i, f: (0, 0, 0)),
        pl.BlockSpec((1, d), lambda i, f: (0, 0)),
        pl.BlockSpec((None, None, d, FF_CHUNK), lambda i, f: (layer, which, 0, f)),
        pl.BlockSpec((None, None, d, FF_CHUNK), lambda i, f: (layer, which, 0, nf + f)),
        pl.BlockSpec((None, None, FF_CHUNK, d), lambda i, f: (layer, which, f, 0)),
    ]
    args = [xs, mod3, norm_g.reshape(1, d), w_in, w_in, w_out]
    if final:
        in_specs.append(pl.BlockSpec((1, d), lambda i, f: (0, 0)))
        args.append(final_g.reshape(1, d))
    return pl.pallas_call(
        functools.partial(_ffn_kernel, n_lat=n_lat, tm=tm, final=final),
        grid=(rows // tm, nf),
        in_specs=in_specs,
        out_specs=pl.BlockSpec((tm, d), lambda i, f: (i, 0)),
        out_shape=jax.ShapeDtypeStruct((rows, d), F32),
        scratch_shapes=[pltpu.VMEM((tm, d), BF16), pltpu.VMEM((tm, d), F32)],
        compiler_params=_cparams(("parallel", "arbitrary")),
        name="ffn",
    )(*args)


def _proj_kernel(x_ref, mod_ref, g_ref, w_ref, o_ref, h_scr, *, n_lat, tm):
    i = pl.program_id(0)

    @pl.when(pl.program_id(1) == 0)
    def _():
        _norm_mod_store(h_scr, x_ref, mod_ref, g_ref, i * tm, tm, n_lat)

    o_ref[...] = jnp.dot(h_scr[...], w_ref[...], preferred_element_type=F32)


def _in_proj(xs, mod3, norm_g, w_in_r, *, tm, n_lat):
    n, d = xs.shape
    tn = P_COLS // 2
    return pl.pallas_call(
        functools.partial(_proj_kernel, n_lat=n_lat, tm=tm),
        grid=(n // tm, P_COLS // tn),
        in_specs=[
            pl.BlockSpec((tm, d), lambda i, j: (i, 0)),
            pl.BlockSpec((3, SUBLANES, d), lambda i, j: (0, 0, 0)),
            pl.BlockSpec((1, d), lambda i, j: (0, 0)),
            pl.BlockSpec((d, tn), lambda i, j: (0, j)),
        ],
        out_specs=pl.BlockSpec((tm, tn), lambda i, j: (i, j)),
        out_shape=jax.ShapeDtypeStruct((n, P_COLS), F32),
        scratch_shapes=[pltpu.VMEM((tm, d), BF16)],
        compiler_params=_cparams(("parallel", "arbitrary")),
        name="in_proj",
    )(xs, mod3, norm_g.reshape(1, d), w_in_r)


def _qkv_kernel(q_ref, kv_ref, cos_ref, sin_ref, g_ref, qo_ref, ko_ref, vo_ref):
    cos = cos_ref[...]
    sin = sin_ref[...]
    lane = lax.broadcasted_iota(jnp.int32, cos.shape, 1)
    first_half = (lane % ROPE_AXIS_DIM) < (ROPE_AXIS_DIM // 2)

    def norm_rope(xh, g):
        xh = _rms(xh, g)
        swapped = jnp.where(first_half, pltpu.roll(xh, LANES - 32, 1), pltpu.roll(xh, 32, 1))
        return xh * cos + swapped * sin

    for h in range(ATTN_HEADS):
        sl = slice(h * HEAD_DIM, (h + 1) * HEAD_DIM)
        qo_ref[:, sl] = (norm_rope(q_ref[:, sl], g_ref[0:1, :]) * QK_EXP2_SCALE).astype(BF16)
    for h in range(ATTN_KV_HEADS):
        sl = slice(h * HEAD_DIM, (h + 1) * HEAD_DIM)
        ko_ref[:, sl] = norm_rope(kv_ref[:, sl], g_ref[1:2, :]).astype(BF16)
    vt = kv_ref[:, KV_WIDTH:].T.astype(BF16)
    ones = jnp.ones((VT_ROWS - HEAD_DIM, vt.shape[1]), BF16)
    for g in range(ATTN_KV_HEADS):
        vo_ref[g * VT_ROWS:g * VT_ROWS + HEAD_DIM, :] = vt[g * HEAD_DIM:(g + 1) * HEAD_DIM, :]
        vo_ref[g * VT_ROWS + HEAD_DIM:(g + 1) * VT_ROWS, :] = ones


def _qkv_prep(p, cos, sin, qk_g):
    n = p.shape[0]
    t = ROW_TILE
    return pl.pallas_call(
        _qkv_kernel,
        grid=(n // t,),
        in_specs=[
            pl.BlockSpec((t, ATTN_WIDTH), lambda i: (i, COL_Q // ATTN_WIDTH)),
            pl.BlockSpec((t, 2 * KV_WIDTH), lambda i: (i, COL_K // (2 * KV_WIDTH))),
            pl.BlockSpec((t, HEAD_DIM), lambda i: (i, 0)),
            pl.BlockSpec((t, HEAD_DIM), lambda i: (i, 0)),
            pl.BlockSpec((2, HEAD_DIM), lambda i: (0, 0)),
        ],
        out_specs=[
            pl.BlockSpec((t, ATTN_WIDTH), lambda i: (i, 0)),
            pl.BlockSpec((t, KV_WIDTH), lambda i: (i, 0)),
            pl.BlockSpec((ATTN_KV_HEADS * VT_ROWS, t), lambda i: (0, i)),
        ],
        out_shape=[
            jax.ShapeDtypeStruct((n, ATTN_WIDTH), BF16),
            jax.ShapeDtypeStruct((n, KV_WIDTH), BF16),
            jax.ShapeDtypeStruct((ATTN_KV_HEADS * VT_ROWS, n), BF16),
        ],
        compiler_params=_cparams(("parallel",)),
        name="qkv_prep",
    )(p, p, cos, sin, qk_g)


def _flash_kernel(*refs, aliased):
    if aliased:
        q_ref, k_ref, vt_ref, _, o_ref, m_scr, acc_scr, *bufs = refs
    else:
        q_ref, k_ref, vt_ref, o_ref, m_scr, acc_scr, *bufs = refs
    s_bufs, p_bufs = bufs[:len(bufs) // 2], bufs[len(bufs) // 2:]
    j = pl.program_id(1)
    tq = q_ref.shape[0]
    tk = k_ref.shape[0]
    w = ATTN_GROUPS * tq
    kc = min(KEY_CHUNK, tk)

    @pl.when(j == 0)
    def _():
        m_scr[...] = jnp.full_like(m_scr, -jnp.inf)
        acc_scr[...] = jnp.zeros_like(acc_scr)

    qs = [jnp.concatenate(
        [q_ref[:, (ATTN_GROUPS * g + h) * HEAD_DIM:(ATTN_GROUPS * g + h + 1) * HEAD_DIM]
         for h in range(ATTN_GROUPS)], axis=0) for g in range(ATTN_KV_HEADS)]
    units = [(g, ci) for ci in range(tk // kc) for g in range(ATTN_KV_HEADS)]

    def scores(u):
        g, ci = units[u]
        st = lax.dot_general(k_ref[ci * kc:(ci + 1) * kc, g * HEAD_DIM:(g + 1) * HEAD_DIM], qs[g],
                             (((1,), (1,)), ((), ())), preferred_element_type=F32)
        s_bufs[u % len(s_bufs)][...] = st
        mx = st[0:SUBLANES]
        for r in range(1, kc // SUBLANES):
            mx = jnp.maximum(mx, st[r * SUBLANES:(r + 1) * SUBLANES])
        return mx

    m_run = [m_scr[g] for g in range(ATTN_KV_HEADS)]
    pending = [scores(u) for u in range(min(SCORE_AHEAD, len(units)))]
    for u, (g, ci) in enumerate(units):
        mx = pending.pop(0)
        if u + SCORE_AHEAD < len(units):
            pending.append(scores(u + SCORE_AHEAD))
        m_new = jnp.maximum(m_run[g], jnp.max(mx, axis=0, keepdims=True))
        alpha = jnp.exp2(m_run[g] - m_new)
        m_rep = jnp.broadcast_to(m_new, (BF16_ROWS, w))
        s_buf, p_buf = s_bufs[u % len(s_bufs)], p_bufs[u % len(p_bufs)]
        for r in range(kc // BF16_ROWS):
            rs = slice(r * BF16_ROWS, (r + 1) * BF16_ROWS)
            p_buf[rs, :] = jnp.exp2(s_buf[rs, :] - m_rep).astype(BF16)
        pv = jnp.dot(vt_ref[g * VT_ROWS:(g + 1) * VT_ROWS, ci * kc:(ci + 1) * kc], p_buf[...],
                     preferred_element_type=F32)
        acc_scr[g] = alpha * acc_scr[g] + pv
        m_run[g] = m_new
    for g in range(ATTN_KV_HEADS):
        m_scr[g] = m_run[g]

    @pl.when(j == pl.num_programs(1) - 1)
    def _():
        for g in range(ATTN_KV_HEADS):
            acc = acc_scr[g]
            ot = acc[0:HEAD_DIM] / acc[HEAD_DIM:HEAD_DIM + 1]
            for h in range(ATTN_GROUPS):
                col = (ATTN_GROUPS * g + h) * HEAD_DIM
                o_ref[:, col:col + HEAD_DIM] = ot[:, h * tq:(h + 1) * tq].T.astype(o_ref.dtype)


def _attention(q, k, vt, *, q_tiles, q_off, kv_tiles, kv_off, tq, tk, prev=None):
    n = q.shape[0]
    aliased = prev is not None
    in_specs = [
        pl.BlockSpec((tq, ATTN_WIDTH), lambda i, j: (q_off + i, 0)),
        pl.BlockSpec((tk, KV_WIDTH), lambda i, j: (kv_off + j, 0)),
        pl.BlockSpec((ATTN_KV_HEADS * VT_ROWS, tk), lambda i, j: (0, kv_off + j)),
    ]
    args = [q, k, vt]
    if aliased:
        in_specs.append(pl.BlockSpec(memory_space=pl.ANY))
        args.append(prev)
    w = ATTN_GROUPS * tq
    kc = min(KEY_CHUNK, tk)
    units = min(ATTN_KV_HEADS * (tk // kc), SCORE_BUFFERS)
    return pl.pallas_call(
        functools.partial(_flash_kernel, aliased=aliased),
        grid=(q_tiles, kv_tiles),
        in_specs=in_specs,
        out_specs=pl.BlockSpec((tq, ATTN_WIDTH), lambda i, j: (q_off + i, 0)),
        out_shape=jax.ShapeDtypeStruct((n, ATTN_WIDTH), BF16),
        scratch_shapes=[
            pltpu.VMEM((ATTN_KV_HEADS, 1, w), F32),
            pltpu.VMEM((ATTN_KV_HEADS, VT_ROWS, w), F32),
        ] + [pltpu.VMEM((kc, w), F32)] * units + [pltpu.VMEM((kc, w), BF16)] * units,
        input_output_aliases={3: 0} if aliased else {},
        compiler_params=_cparams(("parallel", "arbitrary")),
        name="attention_ctx" if aliased else "attention",
    )(*args)


def _segment_edges(i, lat_tiles):
    first = jnp.logical_or(i == 0, i == lat_tiles)
    last = jnp.logical_or(i == lat_tiles - 1, i == pl.num_programs(0) - 1)
    return first, last


def _ssd_conv_kernel(prev_ref, cur_ref, next_ref, w_ref, b_ref, o_ref, ext_scr, *, lat_tiles):
    i = pl.program_id(0)
    t = cur_ref.shape[0]
    first, last = _segment_edges(i, lat_tiles)
    ext_scr[0:SSD_HALO, :] = jnp.where(first, 0.0, prev_ref[...])
    ext_scr[SSD_HALO:SSD_HALO + t, :] = cur_ref[...]
    ext_scr[SSD_HALO + t:, :] = jnp.where(last, 0.0, next_ref[...])
    pad = SSD_CONV // 2
    acc = jnp.broadcast_to(b_ref[...], cur_ref.shape)
    for k in range(SSD_CONV):
        acc = acc + w_ref[k:k + 1, :] * ext_scr[pl.ds(SSD_HALO - pad + k, t), :]
    o_ref[...] = _silu(acc)


def _ssd_conv(p, conv_w, conv_b, *, n_lat):
    n = p.shape[0]
    t = ROW_TILE
    hb = t // SSD_HALO
    last_hb = n // SSD_HALO - 1
    cb = COL_XBC // SSD_CONV_CH
    return pl.pallas_call(
        functools.partial(_ssd_conv_kernel, lat_tiles=n_lat // t),
        grid=(n // t,),
        in_specs=[
            pl.BlockSpec((SSD_HALO, SSD_CONV_CH), lambda i: (jnp.maximum(i * hb - 1, 0), cb)),
            pl.BlockSpec((t, SSD_CONV_CH), lambda i: (i, cb)),
            pl.BlockSpec((SSD_HALO, SSD_CONV_CH), lambda i: (jnp.minimum((i + 1) * hb, last_hb), cb)),
            pl.BlockSpec((SSD_CONV, SSD_CONV_CH), lambda i: (0, 0)),
            pl.BlockSpec((1, SSD_CONV_CH), lambda i: (0, 0)),
        ],
        out_specs=pl.BlockSpec((t, SSD_CONV_CH), lambda i: (i, 0)),
        out_shape=jax.ShapeDtypeStruct((n, SSD_CONV_CH), F32),
        scratch_shapes=[pltpu.VMEM((t + 2 * SSD_HALO, SSD_CONV_CH), F32)],
        compiler_params=_cparams(("parallel",)),
        name="ssd_conv",
    )(p, p, p, conv_w, conv_b.reshape(1, SSD_CONV_CH))


def _cm_kernel(ap_ref, gp_ref, a_ref, g_ref, an_ref, gn_ref, w_ref, b_ref, lg_ref, lb_ref, o_ref, ext_scr,
               *, lat_tiles):
    i = pl.program_id(0)
    t = a_ref.shape[0]
    first, last = _segment_edges(i, lat_tiles)

    def glu(a, gt):
        return a[...] * _sigmoid(gt[...])

    ext_scr[0:CM_HALO, :] = jnp.where(first, 0.0, glu(ap_ref, gp_ref))
    ext_scr[CM_HALO:CM_HALO + t, :] = glu(a_ref, g_ref)
    ext_scr[CM_HALO + t:, :] = jnp.where(last, 0.0, glu(an_ref, gn_ref))
    pad = CM_KERNEL // 2
    acc = jnp.broadcast_to(b_ref[...], a_ref.shape)
    for k in range(CM_KERNEL):
        acc = acc + w_ref[k:k + 1, :] * ext_scr[pl.ds(CM_HALO - pad + k, t), :]
    mu = jnp.mean(acc, axis=-1, keepdims=True)
    cen = acc - mu
    var = jnp.mean(cen * cen, axis=-1, keepdims=True)
    y = cen * lax.rsqrt(var + EPS) * lg_ref[...] + lb_ref[...]
    o_ref[...] = _silu(y).astype(o_ref.dtype)


def _conv_module(p, dw_w, dw_b, ln_g, ln_b, *, n_lat):
    n = p.shape[0]
    t = ROW_TILE
    hb = t // CM_HALO
    last_hb = n // CM_HALO - 1
    ca = COL_GLU // CM_WIDTH
    cg = ca + 1

    def prev_map(c):
        return lambda i: (jnp.maximum(i * hb - 1, 0), c)

    def next_map(c):
        return lambda i: (jnp.minimum((i + 1) * hb, last_hb), c)

    vec = pl.BlockSpec((1, CM_WIDTH), lambda i: (0, 0))
    return pl.pallas_call(
        functools.partial(_cm_kernel, lat_tiles=n_lat // t),
        grid=(n // t,),
        in_specs=[
            pl.BlockSpec((CM_HALO, CM_WIDTH), prev_map(ca)),
            pl.BlockSpec((CM_HALO, CM_WIDTH), prev_map(cg)),
            pl.BlockSpec((t, CM_WIDTH), lambda i: (i, ca)),
            pl.BlockSpec((t, CM_WIDTH), lambda i: (i, cg)),
            pl.BlockSpec((CM_HALO, CM_WIDTH), next_map(ca)),
            pl.BlockSpec((CM_HALO, CM_WIDTH), next_map(cg)),
            pl.BlockSpec((CM_KERNEL, CM_WIDTH), lambda i: (0, 0)),
            vec, vec, vec,
        ],
        out_specs=pl.BlockSpec((t, CM_WIDTH), lambda i: (i, 0)),
        out_shape=jax.ShapeDtypeStruct((n, CM_WIDTH), BF16),
        scratch_shapes=[pltpu.VMEM((t + 2 * CM_HALO, CM_WIDTH), F32)],
        compiler_params=_cparams(("parallel",)),
        name="conv_module",
    )(p, p, p, p, p, p, dw_w, dw_b.reshape(1, -1), ln_g.reshape(1, -1), ln_b.reshape(1, -1))


def _split3_dot(tri, x):
    hi = x.astype(BF16)
    r1 = x - hi.astype(F32)
    mid = r1.astype(BF16)
    lo = (r1 - mid.astype(F32)).astype(BF16)
    return (jnp.dot(tri, hi, preferred_element_type=F32) + jnp.dot(tri, mid, preferred_element_type=F32)
            + jnp.dot(tri, lo, preferred_element_type=F32))


def _ssd_chunk(xa_ref, dt_ref, bias, alog, y_ref, h_scr, backward):
    q = SSD_CHUNK
    x = dt_ref[...] + bias
    dt = jnp.maximum(x, 0.0) + jnp.log1p(jnp.exp(-jnp.abs(x)))
    da = dt * (-jnp.exp(alog))
    ii = lax.broadcasted_iota(jnp.int32, (q, q), 0)
    jj = lax.broadcasted_iota(jnp.int32, (q, q), 1)
    lane = jj
    inc = (jj >= ii) if backward else (jj <= ii)
    tri = jnp.where(inc, 1.0, 0.0).astype(BF16)
    cum = _split3_dot(tri, da)
    tot = jnp.sum(da, axis=0, keepdims=True)
    cum_t = cum.T
    dt_t = dt.T
    e_in = jnp.exp(cum)
    w_out = jnp.exp(tot - cum) * dt
    e_tot = jnp.exp(tot)

    xa = xa_ref[...]
    pair = 2 * SSD_HEAD_DIM
    low = lane < SSD_HEAD_DIM
    for g in range(SSD_GROUPS):
        b = xa[:, SSD_WIDTH + g * SSD_STATE:SSD_WIDTH + (g + 1) * SSD_STATE].astype(BF16)
        cg = xa[:, SSD_WIDTH + (SSD_GROUPS + g) * SSD_STATE:
                SSD_WIDTH + (SSD_GROUPS + g + 1) * SSD_STATE].astype(BF16)
        cb = lax.dot_general(cg, b, (((1,), (1,)), ((), ())), preferred_element_type=F32)
        for m in range(SSD_HEADS // SSD_GROUPS // 2):
            pi = g * (SSD_HEADS // SSD_GROUPS // 2) + m
            h0, h1 = 2 * pi, 2 * pi + 1
            xp = xa[:, pi * pair:(pi + 1) * pair]
            xp_b = xp.astype(BF16)
            ys = []
            for hd in (h0, h1):
                seg = cum[:, hd:hd + 1] - cum_t[hd:hd + 1, :]
                dec = jnp.exp(jnp.where(inc, seg, -jnp.inf))
                sc = (cb * dec * dt_t[hd:hd + 1, :]).astype(BF16)
                ys.append(jnp.dot(sc, xp_b, preferred_element_type=F32))
            y_diag = jnp.where(low, ys[0], ys[1])
            hp = h_scr[pi]
            y_off = lax.dot_general(cg, hp.astype(BF16), (((1,), (1,)), ((), ())),
                                    preferred_element_type=F32)
            y_off = y_off * jnp.where(low, e_in[:, h0:h0 + 1], e_in[:, h1:h1 + 1])
            y_ref[:, pi * pair:(pi + 1) * pair] = y_diag + y_off
            xw = xp * jnp.where(low, w_out[:, h0:h0 + 1], w_out[:, h1:h1 + 1])
            st = jnp.dot(xw.T.astype(BF16), b, preferred_element_type=F32)
            row_low = ii < SSD_HEAD_DIM
            keep = jnp.where(row_low, e_tot[:, h0:h0 + 1], e_tot[:, h1:h1 + 1])
            h_scr[pi] = keep * hp + st


def _ssd_scan_kernel(xaf_ref, xab_ref, dtf_ref, dtb_ref, bias_ref, alog_ref, yf_ref, yb_ref, h_scr):
    @pl.when(pl.program_id(0) == 0)
    def _():
        h_scr[...] = jnp.zeros_like(h_scr)

    _ssd_chunk(xaf_ref, dtf_ref, bias_ref[0], alog_ref[0], yf_ref, h_scr.at[0], False)
    _ssd_chunk(xab_ref, dtb_ref, bias_ref[1], alog_ref[1], yb_ref, h_scr.at[1], True)


def _ssd_scan(xa, p, dt_bias, a_log, *, n_lat):
    n = xa.shape[0]
    nc = n // SSD_CHUNK
    lat_c = n_lat // SSD_CHUNK
    ctx_c = nc - lat_c
    dt_cb = COL_DT // LANES

    def fwd(s):
        return jnp.where(s < ctx_c, lat_c + s, s - ctx_c)

    def bwd(s):
        return nc - 1 - s

    pad = LANES - SSD_HEADS
    bias = jnp.pad(dt_bias, ((0, 0), (0, pad))).reshape(2, 1, LANES)
    alog = jnp.pad(a_log, ((0, 0), (0, pad))).reshape(2, 1, LANES)
    y_sds = jax.ShapeDtypeStruct((n, SSD_WIDTH), F32)
    return pl.pallas_call(
        _ssd_scan_kernel,
        grid=(nc,),
        in_specs=[
            pl.BlockSpec((SSD_CHUNK, SSD_CONV_CH), lambda s: (fwd(s), 0)),
            pl.BlockSpec((SSD_CHUNK, SSD_CONV_CH), lambda s: (bwd(s), 0)),
            pl.BlockSpec((SSD_CHUNK, LANES), lambda s: (fwd(s), dt_cb)),
            pl.BlockSpec((SSD_CHUNK, LANES), lambda s: (bwd(s), dt_cb + 1)),
            pl.BlockSpec((2, 1, LANES), lambda s: (0, 0, 0)),
            pl.BlockSpec((2, 1, LANES), lambda s: (0, 0, 0)),
        ],
        out_specs=[
            pl.BlockSpec((SSD_CHUNK, SSD_WIDTH), lambda s: (fwd(s), 0)),
            pl.BlockSpec((SSD_CHUNK, SSD_WIDTH), lambda s: (bwd(s), 0)),
        ],
        out_shape=[y_sds, y_sds],
        scratch_shapes=[pltpu.VMEM((2, SSD_HEADS // 2, 2 * SSD_HEAD_DIM, SSD_STATE), F32)],
        compiler_params=_cparams(("arbitrary",)),
        name="ssd_scan",
    )(xa, xa, p, p, bias, alog)


def _ssd_out_kernel(yf_ref, yb_ref, xs_ref, z_ref, d_ref, g_ref, o_ref):
    y = yf_ref[...] + yb_ref[...] + d_ref[...] * xs_ref[...]
    y = y * _silu(z_ref[...])
    o_ref[...] = _rms(y, g_ref[...]).astype(o_ref.dtype)


def _ssd_out(y_f, y_b, xa, p, d_skip, norm_g):
    n = xa.shape[0]
    t = ROW_TILE
    vec = pl.BlockSpec((1, SSD_WIDTH), lambda i: (0, 0))
    row = pl.BlockSpec((t, SSD_WIDTH), lambda i: (i, 0))
    return pl.pallas_call(
        _ssd_out_kernel,
        grid=(n // t,),
        in_specs=[
            row, row, row,
            pl.BlockSpec((t, SSD_WIDTH), lambda i: (i, COL_Z // SSD_WIDTH)),
            vec, vec,
        ],
        out_specs=pl.BlockSpec((t, SSD_WIDTH), lambda i: (i, 0)),
        out_shape=jax.ShapeDtypeStruct((n, SSD_WIDTH), BF16),
        compiler_params=_cparams(("parallel",)),
        name="ssd_out",
    )(y_f, y_b, xa, p, jnp.repeat(d_skip, SSD_HEAD_DIM).reshape(1, SSD_WIDTH), norm_g.reshape(1, SSD_WIDTH))


def _out_proj_kernel(x_ref, a_ref, s_ref, c_ref, mod_ref, w_ref, o_ref, *, n_lat, tm):
    i = pl.program_id(0)
    acc = jnp.dot(a_ref[...], w_ref[0:ATTN_WIDTH, :], preferred_element_type=F32)
    acc += jnp.dot(s_ref[...], w_ref[ATTN_WIDTH:ATTN_WIDTH + SSD_WIDTH, :], preferred_element_type=F32)
    acc += jnp.dot(c_ref[...], w_ref[ATTN_WIDTH + SSD_WIDTH:, :], preferred_element_type=F32)
    for r in range(0, tm, ROW_CHUNK):
        rs = slice(r, r + ROW_CHUNK)
        o_ref[rs, :] = x_ref[rs, :] + _mod_row(mod_ref, 2, i * tm + r, n_lat) * acc[rs, :]


def _out_proj(xs, attn, ssd, cm, mod3, w_out, *, rows, tm, n_lat):
    d = xs.shape[1]
    return pl.pallas_call(
        functools.partial(_out_proj_kernel, n_lat=n_lat, tm=tm),
        grid=(rows // tm,),
        in_specs=[
            pl.BlockSpec((tm, d), lambda i: (i, 0)),
            pl.BlockSpec((tm, ATTN_WIDTH), lambda i: (i, 0)),
            pl.BlockSpec((tm, SSD_WIDTH), lambda i: (i, 0)),
            pl.BlockSpec((tm, CM_WIDTH), lambda i: (i, 0)),
            pl.BlockSpec((3, SUBLANES, d), lambda i: (0, 0, 0)),
            pl.BlockSpec((D_MIX, d), lambda i: (0, 0)),
        ],
        out_specs=pl.BlockSpec((tm, d), lambda i: (i, 0)),
        out_shape=jax.ShapeDtypeStruct((rows, d), F32),
        compiler_params=_cparams(("parallel",)),
        name="out_proj",
    )(xs, attn, ssd, cm, mod3, w_out)


def _rope_tables(n_lat, n):
    rows = n_lat // GRID_W
    row = jnp.repeat(jnp.arange(rows), GRID_W).astype(F32)
    col = jnp.tile(jnp.arange(GRID_W), rows).astype(F32)
    inv = ROPE_THETA ** (-jnp.arange(0, ROPE_AXIS_DIM, 2, dtype=F32) / ROPE_AXIS_DIM)
    ar = row[:, None] * inv
    ac = col[:, None] * inv
    cos = jnp.concatenate([jnp.cos(ar), jnp.cos(ar), jnp.cos(ac), jnp.cos(ac)], axis=1)
    sin = jnp.concatenate([-jnp.sin(ar), jnp.sin(ar), -jnp.sin(ac), jnp.sin(ac)], axis=1)
    pad = n - n_lat
    cos = jnp.concatenate([cos, jnp.ones((pad, HEAD_DIM), F32)], axis=0)
    sin = jnp.concatenate([sin, jnp.zeros((pad, HEAD_DIM), F32)], axis=0)
    return cos, sin


def _relayout_w_in(w_in):
    d = w_in.shape[0]
    o_dt = COL_GLU
    main = w_in[:, :o_dt]
    dt = w_in[:, o_dt:o_dt + 2 * SSD_HEADS]
    glu = w_in[:, o_dt + 2 * SSD_HEADS:]
    zpad = jnp.zeros((d, LANES - SSD_HEADS), w_in.dtype)
    return jnp.concatenate([main, glu, dt[:, :SSD_HEADS], zpad, dt[:, SSD_HEADS:], zpad], axis=1).astype(BF16)


def kernel(x, c, ctx, c_ctx, w_mod, b_mod, norm_g, w_ffn_in, w_ffn_out, w_in, w_out, qk_g,
           ssd_conv_w, ssd_conv_b, ssd_dt_bias, ssd_a_log, ssd_d, ssd_norm_g,
           cm_dw_w, cm_dw_b, cm_ln_g, cm_ln_b, final_g):
    assert x.shape[0] == 1 and ctx.shape[0] == 1
    n_lat, d = x.shape[1], x.shape[2]
    n_ctx = ctx.shape[1]
    n = n_lat + n_ctx
    depth = w_mod.shape[0]
    assert n % MM_TILE == 0 and n_lat % ROW_TILE == 0 and n_ctx % ROW_TILE == 0
    assert ROW_TILE % ROW_CHUNK == 0 and MM_TILE % (2 * ROW_CHUNK) == 0
    kv_tile = max(t for t in range(KEY_CHUNK, KV_TILE + 1, KEY_CHUNK) if n % t == 0)
    lat_tm = 512
    assert n_lat % lat_tm == 0 and n_lat % GRID_W == 0

    xs = jnp.concatenate([x[0], ctx[0]], axis=0)
    cvec = jnp.concatenate([c_ctx[None, :], c, jnp.zeros((SUBLANES - 2, d), F32)], axis=0)
    mods = _modulation(cvec, w_mod, b_mod)
    mods = mods.reshape(depth, SUBLANES, N_MOD, d).transpose(0, 2, 1, 3)
    cos, sin = _rope_tables(n_lat, n)
    w_ffn_in_b = w_ffn_in.astype(BF16)
    w_ffn_out_b = w_ffn_out.astype(BF16)
    w_out_b = w_out.astype(BF16)

    out = None
    for l in range(depth):
        last = l == depth - 1
        m = mods[l]
        xs = _ffn(xs, m[0:3], norm_g[l, 0], w_ffn_in_b, w_ffn_out_b, l, 0,
                  rows=n, tm=MM_TILE, n_lat=n_lat)
        p = _in_proj(xs, m[3:6], norm_g[l, 1], _relayout_w_in(w_in[l]), tm=MM_TILE, n_lat=n_lat)
        q, k, v = _qkv_prep(p, cos, sin, qk_g[l])
        attn = _attention(q, k, v, q_tiles=n_lat // Q_TILE, q_off=0, kv_tiles=n // kv_tile, kv_off=0,
                          tq=Q_TILE, tk=kv_tile)
        if not last:
            attn = _attention(q, k, v, q_tiles=n_ctx // ROW_TILE, q_off=n_lat // ROW_TILE,
                              kv_tiles=n_ctx // ROW_TILE, kv_off=n_lat // ROW_TILE,
                              tq=ROW_TILE, tk=ROW_TILE, prev=attn)
        xa = _ssd_conv(p, ssd_conv_w[l], ssd_conv_b[l], n_lat=n_lat)
        y_f, y_b = _ssd_scan(xa, p, ssd_dt_bias[l], ssd_a_log[l], n_lat=n_lat)
        ssd = _ssd_out(y_f, y_b, xa, p, ssd_d[l], ssd_norm_g[l])
        cm = _conv_module(p, cm_dw_w[l], cm_dw_b[l], cm_ln_g[l], cm_ln_b[l], n_lat=n_lat)
        if last:
            xs = _out_proj(xs, attn, ssd, cm, m[3:6], w_out_b[l], rows=n_lat, tm=lat_tm // 2, n_lat=n_lat)
            out = _ffn(xs, m[6:9], norm_g[l, 2], w_ffn_in_b, w_ffn_out_b, l, 1,
                       rows=n_lat, tm=lat_tm, n_lat=n_lat, final_g=final_g)
        else:
            xs = _out_proj(xs, attn, ssd, cm, m[3:6], w_out_b[l], rows=n, tm=MM_TILE // 2, n_lat=n_lat)
            xs = _ffn(xs, m[6:9], norm_g[l, 2], w_ffn_in_b, w_ffn_out_b, l, 1,
                      rows=n, tm=MM_TILE, n_lat=n_lat)
    return out[None]
```

```python
import functools
import math

import jax
import jax.numpy as jnp
from jax import lax
from jax.experimental import pallas as pl
from jax.experimental.pallas import tpu as pltpu

F32 = jnp.float32
BF16 = jnp.bfloat16

D_MODEL = 2048
DEPTH = 2
GRID_W = 64
N_MOD = 9
D_FF = 5632
EPS = 1e-6

HEAD_DIM = 128
ATTN_HEADS = 8
ATTN_KV_HEADS = 2
ATTN_GROUPS = ATTN_HEADS // ATTN_KV_HEADS
ATTN_WIDTH = ATTN_HEADS * HEAD_DIM
KV_WIDTH = ATTN_KV_HEADS * HEAD_DIM
ROPE_THETA = 10000.0
ROPE_AXIS_DIM = HEAD_DIM // 2

SSD_HEADS = 8
SSD_HEAD_DIM = 64
SSD_WIDTH = SSD_HEADS * SSD_HEAD_DIM
SSD_GROUPS = 2
SSD_STATE = 128
SSD_CONV = 7
SSD_CHUNK = 128
SSD_CONV_CH = SSD_WIDTH + 2 * SSD_GROUPS * SSD_STATE

CM_WIDTH = 512
CM_KERNEL = 31
D_MIX = ATTN_WIDTH + SSD_WIDTH + CM_WIDTH

LANES = 128
SUBLANES = 8
VMEM_LIMIT = 56 * 1024 * 1024

COL_Q = 0
COL_K = COL_Q + ATTN_WIDTH
COL_V = COL_K + KV_WIDTH
COL_Z = COL_V + KV_WIDTH
COL_XBC = COL_Z + SSD_WIDTH
COL_GLU = COL_XBC + SSD_CONV_CH
COL_DT = COL_GLU + 2 * CM_WIDTH
P_COLS = COL_DT + 2 * LANES

ROW_TILE = 256
MM_TILE = 640
FF_CHUNK = 512
KV_TILE = 3328
Q_TILE = 256
SSD_HALO = 8
CM_HALO = 16
BF16_ROWS = 16
VT_ROWS = HEAD_DIM + BF16_ROWS
QK_EXP2_SCALE = (HEAD_DIM ** -0.5) * math.log2(math.e)
KEY_CHUNK = 256
SCORE_BUFFERS = 3
SCORE_AHEAD = 1


def _cparams(sem):
    return pltpu.CompilerParams(dimension_semantics=sem, vmem_limit_bytes=VMEM_LIMIT)


def _sigmoid(x):
    return 1.0 / (1.0 + jnp.exp(-x))


def _silu(x):
    return x * _sigmoid(x)


def _rms(x, g):
    return x * lax.rsqrt(jnp.mean(x * x, axis=-1, keepdims=True) + EPS) * g


ROW_CHUNK = 64


def _mod_row(mod_ref, k, row0, n_lat):
    cls = (row0 < n_lat).astype(jnp.int32)
    return mod_ref[k, pl.ds(cls, 1), :]


def _norm_mod_store(h_scr, x_ref, mod_ref, g_ref, row0, tm, n_lat):
    for r in range(0, tm, ROW_CHUNK):
        shift = _mod_row(mod_ref, 0, row0 + r, n_lat)
        gain = g_ref[...] * (1.0 + _mod_row(mod_ref, 1, row0 + r, n_lat))
        x = x_ref[r:r + ROW_CHUNK, :]
        inv = lax.rsqrt(jnp.mean(x * x, axis=-1, keepdims=True) + EPS)
        h_scr[r:r + ROW_CHUNK, :] = ((x * inv) * gain + shift).astype(BF16)


def _mod_kernel(c_ref, w_ref, b_ref, o_ref):
    a = _silu(c_ref[...]).astype(BF16)
    o_ref[0] = jnp.dot(a, w_ref[0].astype(BF16), preferred_element_type=F32) + b_ref[0]


def _modulation(cvec, w_mod, b_mod):
    depth, d, n = w_mod.shape
    tn = 1024
    return pl.pallas_call(
        _mod_kernel,
        grid=(depth, n // tn),
        in_specs=[
            pl.BlockSpec((SUBLANES, d), lambda l, j: (0, 0)),
            pl.BlockSpec((1, d, tn), lambda l, j: (l, 0, j)),
            pl.BlockSpec((1, 1, tn), lambda l, j: (l, 0, j)),
        ],
        out_specs=pl.BlockSpec((1, SUBLANES, tn), lambda l, j: (l, 0, j)),
        out_shape=jax.ShapeDtypeStruct((depth, SUBLANES, n), F32),
        compiler_params=_cparams(("arbitrary", "arbitrary")),
        name="modulation",
    )(cvec, w_mod, b_mod.reshape(depth, 1, n))


def _ffn_kernel(x_ref, mod_ref, g_ref, wg_ref, wu_ref, wo_ref, *rest, n_lat, tm, final):
    if final:
        fg_ref, o_ref, h_scr, acc_scr = rest
    else:
        o_ref, h_scr, acc_scr = rest
    i = pl.program_id(0)
    f = pl.program_id(1)

    @pl.when(f == 0)
    def _():
        _norm_mod_store(h_scr, x_ref, mod_ref, g_ref, i * tm, tm, n_lat)
        acc_scr[...] = jnp.zeros_like(acc_scr)

    h = h_scr[...]
    gt = jnp.dot(h, wg_ref[...], preferred_element_type=F32)
    up = jnp.dot(h, wu_ref[...], preferred_element_type=F32)
    a = (_silu(gt) * up).astype(BF16)
    acc_scr[...] += jnp.dot(a, wo_ref[...], preferred_element_type=F32)

    @pl.when(f == pl.num_programs(1) - 1)
    def _():
        for r in range(0, tm, ROW_CHUNK):
            rs = slice(r, r + ROW_CHUNK)
            gate = _mod_row(mod_ref, 2, i * tm + r, n_lat)
            y = x_ref[rs, :] + (0.5 * gate) * acc_scr[rs, :]
            if final:
                y = _rms(y, fg_ref[...])
            o_ref[rs, :] = y


def _ffn(xs, mod3, norm_g, w_in, w_out, layer, which, *, rows, tm, n_lat, final_g=None):
    d = xs.shape[1]
    nf = D_FF // FF_CHUNK
    final = final_g is not None
    in_specs = [
        pl.BlockSpec((tm, d), lambda i, f: (i, 0)),
        pl.BlockSpec((3, SUBLANES, d), lambda i, f: (0, 0, 0)),
        pl.BlockSpec((1, d), lambda i, f: (0, 0)),
        pl.BlockSpec((None, None, d, FF_CHUNK), lambda i, f: (layer, which, 0, f)),
        pl.BlockSpec((None, None, d, FF_CHUNK), lambda i, f: (layer, which, 0, nf + f)),
        pl.BlockSpec((None, None, FF_CHUNK, d), lambda i, f: (layer, which, f, 0)),
    ]
    args = [xs, mod3, norm_g.reshape(1, d), w_in, w_in, w_out]
    if final:
        in_specs.append(pl.BlockSpec((1, d), lambda i, f: (0, 0)))
        args.append(final_g.reshape(1, d))
    return pl.pallas_call(
        functools.partial(_ffn_kernel, n_lat=n_lat, tm=tm, final=final),
        grid=(rows // tm, nf),
        in_specs=in_specs,
        out_specs=pl.BlockSpec((tm, d), lambda i, f: (i, 0)),
        out_shape=jax.ShapeDtypeStruct((rows, d), F32),
        scratch_shapes=[pltpu.VMEM((tm, d), BF16), pltpu.VMEM((tm, d), F32)],
        compiler_params=_cparams(("parallel", "arbitrary")),
        name="ffn",
    )(*args)


def _proj_kernel(x_ref, mod_ref, g_ref, w_ref, o_ref, h_scr, *, n_lat, tm):
    i = pl.program_id(0)

    @pl.when(pl.program_id(1) == 0)
    def _():
        _norm_mod_store(h_scr, x_ref, mod_ref, g_ref, i * tm, tm, n_lat)

    o_ref[...] = jnp.dot(h_scr[...], w_ref[...], preferred_element_type=F32)


def _in_proj(xs, mod3, norm_g, w_in_r, *, tm, n_lat):
    n, d = xs.shape
    tn = P_COLS // 2
    return pl.pallas_call(
        functools.partial(_proj_kernel, n_lat=n_lat, tm=tm),
        grid=(n // tm, P_COLS // tn),
        in_specs=[
            pl.BlockSpec((tm, d), lambda i, j: (i, 0)),
            pl.BlockSpec((3, SUBLANES, d), lambda i, j: (0, 0, 0)),
            pl.BlockSpec((1, d), lambda i, j: (0, 0)),
            pl.BlockSpec((d, tn), lambda i, j: (0, j)),
        ],
        out_specs=pl.BlockSpec((tm, tn), lambda i, j: (i, j)),
        out_shape=jax.ShapeDtypeStruct((n, P_COLS), F32),
        scratch_shapes=[pltpu.VMEM((tm, d), BF16)],
        compiler_params=_cparams(("parallel", "arbitrary")),
        name="in_proj",
    )(xs, mod3, norm_g.reshape(1, d), w_in_r)


def _qkv_kernel(q_ref, kv_ref, cos_ref, sin_ref, g_ref, qo_ref, ko_ref, vo_ref):
    cos = cos_ref[...]
    sin = sin_ref[...]
    lane = lax.broadcasted_iota(jnp.int32, cos.shape, 1)
    first_half = (lane % ROPE_AXIS_DIM) < (ROPE_AXIS_DIM // 2)

    def norm_rope(xh, g):
        xh = _rms(xh, g)
        swapped = jnp.where(first_half, pltpu.roll(xh, LANES - 32, 1), pltpu.roll(xh, 32, 1))
        return xh * cos + swapped * sin

    for h in range(ATTN_HEADS):
        sl = slice(h * HEAD_DIM, (h + 1) * HEAD_DIM)
        qo_ref[:, sl] = (norm_rope(q_ref[:, sl], g_ref[0:1, :]) * QK_EXP2_SCALE).astype(BF16)
    for h in range(ATTN_KV_HEADS):
        sl = slice(h * HEAD_DIM, (h + 1) * HEAD_DIM)
        ko_ref[:, sl] = norm_rope(kv_ref[:, sl], g_ref[1:2, :]).astype(BF16)
    vt = kv_ref[:, KV_WIDTH:].T.astype(BF16)
    ones = jnp.ones((VT_ROWS - HEAD_DIM, vt.shape[1]), BF16)
    for g in range(ATTN_KV_HEADS):
        vo_ref[g * VT_ROWS:g * VT_ROWS + HEAD_DIM, :] = vt[g * HEAD_DIM:(g + 1) * HEAD_DIM, :]
        vo_ref[g * VT_ROWS + HEAD_DIM:(g + 1) * VT_ROWS, :] = ones


def _qkv_prep(p, cos, sin, qk_g):
    n = p.shape[0]
    t = ROW_TILE
    return pl.pallas_call(
        _qkv_kernel,
        grid=(n // t,),
        in_specs=[
            pl.BlockSpec((t, ATTN_WIDTH), lambda i: (i, COL_Q // ATTN_WIDTH)),
            pl.BlockSpec((t, 2 * KV_WIDTH), lambda i: (i, COL_K // (2 * KV_WIDTH))),
            pl.BlockSpec((t, HEAD_DIM), lambda i: (i, 0)),
            pl.BlockSpec((t, HEAD_DIM), lambda i: (i, 0)),
            pl.BlockSpec((2, HEAD_DIM), lambda i: (0, 0)),
        ],
        out_specs=[
            pl.BlockSpec((t, ATTN_WIDTH), lambda i: (i, 0)),
            pl.BlockSpec((t, KV_WIDTH), lambda i: (i, 0)),
            pl.BlockSpec((ATTN_KV_HEADS * VT_ROWS, t), lambda i: (0, i)),
        ],
        out_shape=[
            jax.ShapeDtypeStruct((n, ATTN_WIDTH), BF16),
            jax.ShapeDtypeStruct((n, KV_WIDTH), BF16),
            jax.ShapeDtypeStruct((ATTN_KV_HEADS * VT_ROWS, n), BF16),
        ],
        compiler_params=_cparams(("parallel",)),
        name="qkv_prep",
    )(p, p, cos, sin, qk_g)


def _flash_kernel(*refs, aliased):
    if aliased:
        q_ref, k_ref, vt_ref, _, o_ref, m_scr, acc_scr, *bufs = refs
    else:
        q_ref, k_ref, vt_ref, o_ref, m_scr, acc_scr, *bufs = refs
    s_bufs, p_bufs = bufs[:len(bufs) // 2], bufs[len(bufs) // 2:]
    j = pl.program_id(1)
    tq = q_ref.shape[0]
    tk = k_ref.shape[0]
    w = ATTN_GROUPS * tq
    kc = min(KEY_CHUNK, tk)

    @pl.when(j == 0)
    def _():
        m_scr[...] = jnp.full_like(m_scr, -jnp.inf)
        acc_scr[...] = jnp.zeros_like(acc_scr)

    qs = [jnp.concatenate(
        [q_ref[:, (ATTN_GROUPS * g + h) * HEAD_DIM:(ATTN_GROUPS * g + h + 1) * HEAD_DIM]
         for h in range(ATTN_GROUPS)], axis=0) for g in range(ATTN_KV_HEADS)]
    units = [(g, ci) for ci in range(tk // kc) for g in range(ATTN_KV_HEADS)]

    def scores(u):
        g, ci = units[u]
        st = lax.dot_general(k_ref[ci * kc:(ci + 1) * kc, g * HEAD_DIM:(g + 1) * HEAD_DIM], qs[g],
                             (((1,), (1,)), ((), ())), preferred_element_type=F32)
        s_bufs[u % len(s_bufs)][...] = st
        mx = st[0:SUBLANES]
        for r in range(1, kc // SUBLANES):
            mx = jnp.maximum(mx, st[r * SUBLANES:(r + 1) * SUBLANES])
        return mx

    m_run = [m_scr[g] for g in range(ATTN_KV_HEADS)]
    row0 = jnp.minimum(j, 0)
    pending = [scores(u) for u in range(min(SCORE_AHEAD, len(units)))]
    for u, (g, ci) in enumerate(units):
        mx = pending.pop(0)
        if u + SCORE_AHEAD < len(units):
            pending.append(scores(u + SCORE_AHEAD))
        m_new = jnp.maximum(m_run[g], jnp.max(mx, axis=0, keepdims=True))
        alpha = jnp.exp2(m_run[g] - m_new)
        m_rep = jnp.broadcast_to(m_new, (BF16_ROWS, w))
        s_buf, p_buf = s_bufs[u % len(s_bufs)], p_bufs[u % len(p_bufs)]
        for r in range(kc // BF16_ROWS):
            rs = slice(r * BF16_ROWS, (r + 1) * BF16_ROWS)
            rd = pl.ds(pl.multiple_of(row0 + r * BF16_ROWS, BF16_ROWS), BF16_ROWS)
            p_buf[rs, :] = jnp.exp2(s_buf[rd, :] - m_rep).astype(BF16)
        pv = jnp.dot(vt_ref[g * VT_ROWS:(g + 1) * VT_ROWS, ci * kc:(ci + 1) * kc], p_buf[...],
                     preferred_element_type=F32)
        acc_scr[g] = alpha * acc_scr[g] + pv
        m_run[g] = m_new
    for g in range(ATTN_KV_HEADS):
        m_scr[g] = m_run[g]

    @pl.when(j == pl.num_programs(1) - 1)
    def _():
        for g in range(ATTN_KV_HEADS):
            acc = acc_scr[g]
            ot = acc[0:HEAD_DIM] / acc[HEAD_DIM:HEAD_DIM + 1]
            for h in range(ATTN_GROUPS):
                col = (ATTN_GROUPS * g + h) * HEAD_DIM
                o_ref[:, col:col + HEAD_DIM] = ot[:, h * tq:(h + 1) * tq].T.astype(o_ref.dtype)


def _attention(q, k, vt, *, q_tiles, q_off, kv_tiles, kv_off, tq, tk, prev=None):
    n = q.shape[0]
    aliased = prev is not None
    in_specs = [
        pl.BlockSpec((tq, ATTN_WIDTH), lambda i, j: (q_off + i, 0)),
        pl.BlockSpec((tk, KV_WIDTH), lambda i, j: (kv_off + j, 0)),
        pl.BlockSpec((ATTN_KV_HEADS * VT_ROWS, tk), lambda i, j: (0, kv_off + j)),
    ]
    args = [q, k, vt]
    if aliased:
        in_specs.append(pl.BlockSpec(memory_space=pl.ANY))
        args.append(prev)
    w = ATTN_GROUPS * tq
    kc = min(KEY_CHUNK, tk)
    units = min(ATTN_KV_HEADS * (tk // kc), SCORE_BUFFERS)
    return pl.pallas_call(
        functools.partial(_flash_kernel, aliased=aliased),
        grid=(q_tiles, kv_tiles),
        in_specs=in_specs,
        out_specs=pl.BlockSpec((tq, ATTN_WIDTH), lambda i, j: (q_off + i, 0)),
        out_shape=jax.ShapeDtypeStruct((n, ATTN_WIDTH), BF16),
        scratch_shapes=[
            pltpu.VMEM((ATTN_KV_HEADS, 1, w), F32),
            pltpu.VMEM((ATTN_KV_HEADS, VT_ROWS, w), F32),
        ] + [pltpu.VMEM((kc, w), F32)] * units + [pltpu.VMEM((kc, w), BF16)] * units,
        input_output_aliases={3: 0} if aliased else {},
        compiler_params=_cparams(("parallel", "arbitrary")),
        name="attention_ctx" if aliased else "attention",
    )(*args)


def _segment_edges(i, lat_tiles):
    first = jnp.logical_or(i == 0, i == lat_tiles)
    last = jnp.logical_or(i == lat_tiles - 1, i == pl.num_programs(0) - 1)
    return first, last


def _ssd_conv_kernel(prev_ref, cur_ref, next_ref, w_ref, b_ref, o_ref, ext_scr, *, lat_tiles):
    i = pl.program_id(0)
    t = cur_ref.shape[0]
    first, last = _segment_edges(i, lat_tiles)
    ext_scr[0:SSD_HALO, :] = jnp.where(first, 0.0, prev_ref[...])
    ext_scr[SSD_HALO:SSD_HALO + t, :] = cur_ref[...]
    ext_scr[SSD_HALO + t:, :] = jnp.where(last, 0.0, next_ref[...])
    pad = SSD_CONV // 2
    acc = jnp.broadcast_to(b_ref[...], cur_ref.shape)
    for k in range(SSD_CONV):
        acc = acc + w_ref[k:k + 1, :] * ext_scr[pl.ds(SSD_HALO - pad + k, t), :]
    o_ref[...] = _silu(acc)


def _ssd_conv(p, conv_w, conv_b, *, n_lat):
    n = p.shape[0]
    t = ROW_TILE
    hb = t // SSD_HALO
    last_hb = n // SSD_HALO - 1
    cb = COL_XBC // SSD_CONV_CH
    return pl.pallas_call(
        functools.partial(_ssd_conv_kernel, lat_tiles=n_lat // t),
        grid=(n // t,),
        in_specs=[
            pl.BlockSpec((SSD_HALO, SSD_CONV_CH), lambda i: (jnp.maximum(i * hb - 1, 0), cb)),
            pl.BlockSpec((t, SSD_CONV_CH), lambda i: (i, cb)),
            pl.BlockSpec((SSD_HALO, SSD_CONV_CH), lambda i: (jnp.minimum((i + 1) * hb, last_hb), cb)),
            pl.BlockSpec((SSD_CONV, SSD_CONV_CH), lambda i: (0, 0)),
            pl.BlockSpec((1, SSD_CONV_CH), lambda i: (0, 0)),
        ],
        out_specs=pl.BlockSpec((t, SSD_CONV_CH), lambda i: (i, 0)),
        out_shape=jax.ShapeDtypeStruct((n, SSD_CONV_CH), F32),
        scratch_shapes=[pltpu.VMEM((t + 2 * SSD_HALO, SSD_CONV_CH), F32)],
        compiler_params=_cparams(("parallel",)),
        name="ssd_conv",
    )(p, p, p, conv_w, conv_b.reshape(1, SSD_CONV_CH))


def _cm_kernel(ap_ref, gp_ref, a_ref, g_ref, an_ref, gn_ref, w_ref, b_ref, lg_ref, lb_ref, o_ref, ext_scr,
               sh_scr, *, lat_tiles):
    i = pl.program_id(0)
    t = a_ref.shape[0]
    first, last = _segment_edges(i, lat_tiles)

    def glu(a, gt):
        return a[...] * _sigmoid(gt[...])

    ext_scr[0:CM_HALO, :] = jnp.where(first, 0.0, glu(ap_ref, gp_ref))
    ext_scr[CM_HALO:CM_HALO + t, :] = glu(a_ref, g_ref)
    ext_scr[CM_HALO + t:, :] = jnp.where(last, 0.0, glu(an_ref, gn_ref))
    pad = CM_KERNEL // 2
    first_off = CM_HALO - pad
    span = t + (first_off + CM_KERNEL - 1) // SUBLANES * SUBLANES
    acc = jnp.broadcast_to(b_ref[...], a_ref.shape)
    for phase in range(SUBLANES):
        taps = [k for k in range(CM_KERNEL) if (first_off + k) % SUBLANES == phase]
        if not taps:
            continue
        shifted = sh_scr.at[phase % 2]
        shifted[...] = ext_scr[phase:phase + span, :]
        for k in taps:
            row = first_off + k - phase
            acc = acc + w_ref[k:k + 1, :] * shifted[row:row + t, :]
    mu = jnp.mean(acc, axis=-1, keepdims=True)
    cen = acc - mu
    var = jnp.mean(cen * cen, axis=-1, keepdims=True)
    y = cen * lax.rsqrt(var + EPS) * lg_ref[...] + lb_ref[...]
    o_ref[...] = _silu(y).astype(o_ref.dtype)


def _conv_module(p, dw_w, dw_b, ln_g, ln_b, *, n_lat):
    n = p.shape[0]
    t = ROW_TILE
    hb = t // CM_HALO
    last_hb = n // CM_HALO - 1
    ca = COL_GLU // CM_WIDTH
    cg = ca + 1

    def prev_map(c):
        return lambda i: (jnp.maximum(i * hb - 1, 0), c)

    def next_map(c):
        return lambda i: (jnp.minimum((i + 1) * hb, last_hb), c)

    vec = pl.BlockSpec((1, CM_WIDTH), lambda i: (0, 0))
    return pl.pallas_call(
        functools.partial(_cm_kernel, lat_tiles=n_lat // t),
        grid=(n // t,),
        in_specs=[
            pl.BlockSpec((CM_HALO, CM_WIDTH), prev_map(ca)),
            pl.BlockSpec((CM_HALO, CM_WIDTH), prev_map(cg)),
            pl.BlockSpec((t, CM_WIDTH), lambda i: (i, ca)),
            pl.BlockSpec((t, CM_WIDTH), lambda i: (i, cg)),
            pl.BlockSpec((CM_HALO, CM_WIDTH), next_map(ca)),
            pl.BlockSpec((CM_HALO, CM_WIDTH), next_map(cg)),
            pl.BlockSpec((CM_KERNEL, CM_WIDTH), lambda i: (0, 0)),
            vec, vec, vec,
        ],
        out_specs=pl.BlockSpec((t, CM_WIDTH), lambda i: (i, 0)),
        out_shape=jax.ShapeDtypeStruct((n, CM_WIDTH), BF16),
        scratch_shapes=[pltpu.VMEM((t + 2 * CM_HALO, CM_WIDTH), F32),
                        pltpu.VMEM((2, t + 2 * CM_HALO - SUBLANES, CM_WIDTH), F32)],
        compiler_params=_cparams(("parallel",)),
        name="conv_module",
    )(p, p, p, p, p, p, dw_w, dw_b.reshape(1, -1), ln_g.reshape(1, -1), ln_b.reshape(1, -1))


def _split3_dot(tri, x):
    hi = x.astype(BF16)
    r1 = x - hi.astype(F32)
    mid = r1.astype(BF16)
    lo = (r1 - mid.astype(F32)).astype(BF16)
    return (jnp.dot(tri, hi, preferred_element_type=F32) + jnp.dot(tri, mid, preferred_element_type=F32)
            + jnp.dot(tri, lo, preferred_element_type=F32))


def _ssd_chunk(xa_ref, dt_ref, bias, alog, y_ref, h_scr, backward):
    q = SSD_CHUNK
    x = dt_ref[...] + bias
    dt = jnp.maximum(x, 0.0) + jnp.log1p(jnp.exp(-jnp.abs(x)))
    da = dt * (-jnp.exp(alog))
    ii = lax.broadcasted_iota(jnp.int32, (q, q), 0)
    jj = lax.broadcasted_iota(jnp.int32, (q, q), 1)
    lane = jj
    inc = (jj >= ii) if backward else (jj <= ii)
    tri = jnp.where(inc, 1.0, 0.0).astype(BF16)
    cum = _split3_dot(tri, da)
    tot = jnp.sum(da, axis=0, keepdims=True)
    cum_t = cum.T
    dt_t = dt.T
    e_in = jnp.exp(cum)
    w_out = jnp.exp(tot - cum) * dt
    e_tot = jnp.exp(tot)

    xa = xa_ref[...]
    pair = 2 * SSD_HEAD_DIM
    low = lane < SSD_HEAD_DIM
    for g in range(SSD_GROUPS):
        b = xa[:, SSD_WIDTH + g * SSD_STATE:SSD_WIDTH + (g + 1) * SSD_STATE].astype(BF16)
        cg = xa[:, SSD_WIDTH + (SSD_GROUPS + g) * SSD_STATE:
                SSD_WIDTH + (SSD_GROUPS + g + 1) * SSD_STATE].astype(BF16)
        cb = lax.dot_general(cg, b, (((1,), (1,)), ((), ())), preferred_element_type=F32)
        for m in range(SSD_HEADS // SSD_GROUPS // 2):
            pi = g * (SSD_HEADS // SSD_GROUPS // 2) + m
            h0, h1 = 2 * pi, 2 * pi + 1
            xp = xa[:, pi * pair:(pi + 1) * pair]
            xp_b = xp.astype(BF16)
            ys = []
            for hd in (h0, h1):
                seg = cum[:, hd:hd + 1] - cum_t[hd:hd + 1, :]
                dec = jnp.exp(jnp.where(inc, seg, -jnp.inf))
                sc = (cb * dec * dt_t[hd:hd + 1, :]).astype(BF16)
                ys.append(jnp.dot(sc, xp_b, preferred_element_type=F32))
            y_diag = jnp.where(low, ys[0], ys[1])
            hp = h_scr[pi]
            y_off = lax.dot_general(cg, hp.astype(BF16), (((1,), (1,)), ((), ())),
                                    preferred_element_type=F32)
            y_off = y_off * jnp.where(low, e_in[:, h0:h0 + 1], e_in[:, h1:h1 + 1])
            y_ref[:, pi * pair:(pi + 1) * pair] = y_diag + y_off
            xw = xp * jnp.where(low, w_out[:, h0:h0 + 1], w_out[:, h1:h1 + 1])
            st = jnp.dot(xw.T.astype(BF16), b, preferred_element_type=F32)
            row_low = ii < SSD_HEAD_DIM
            keep = jnp.where(row_low, e_tot[:, h0:h0 + 1], e_tot[:, h1:h1 + 1])
            h_scr[pi] = keep * hp + st


def _ssd_scan_kernel(xaf_ref, xab_ref, dtf_ref, dtb_ref, bias_ref, alog_ref, yf_ref, yb_ref, h_scr):
    @pl.when(pl.program_id(0) == 0)
    def _():
        h_scr[...] = jnp.zeros_like(h_scr)

    _ssd_chunk(xaf_ref, dtf_ref, bias_ref[0], alog_ref[0], yf_ref, h_scr.at[0], False)
    _ssd_chunk(xab_ref, dtb_ref, bias_ref[1], alog_ref[1], yb_ref, h_scr.at[1], True)


def _ssd_scan(xa, p, dt_bias, a_log, *, n_lat):
    n = xa.shape[0]
    nc = n // SSD_CHUNK
    lat_c = n_lat // SSD_CHUNK
    ctx_c = nc - lat_c
    dt_cb = COL_DT // LANES

    def fwd(s):
        return jnp.where(s < ctx_c, lat_c + s, s - ctx_c)

    def bwd(s):
        return nc - 1 - s

    pad = LANES - SSD_HEADS
    bias = jnp.pad(dt_bias, ((0, 0), (0, pad))).reshape(2, 1, LANES)
    alog = jnp.pad(a_log, ((0, 0), (0, pad))).reshape(2, 1, LANES)
    y_sds = jax.ShapeDtypeStruct((n, SSD_WIDTH), F32)
    return pl.pallas_call(
        _ssd_scan_kernel,
        grid=(nc,),
        in_specs=[
            pl.BlockSpec((SSD_CHUNK, SSD_CONV_CH), lambda s: (fwd(s), 0)),
            pl.BlockSpec((SSD_CHUNK, SSD_CONV_CH), lambda s: (bwd(s), 0)),
            pl.BlockSpec((SSD_CHUNK, LANES), lambda s: (fwd(s), dt_cb)),
            pl.BlockSpec((SSD_CHUNK, LANES), lambda s: (bwd(s), dt_cb + 1)),
            pl.BlockSpec((2, 1, LANES), lambda s: (0, 0, 0)),
            pl.BlockSpec((2, 1, LANES), lambda s: (0, 0, 0)),
        ],
        out_specs=[
            pl.BlockSpec((SSD_CHUNK, SSD_WIDTH), lambda s: (fwd(s), 0)),
            pl.BlockSpec((SSD_CHUNK, SSD_WIDTH), lambda s: (bwd(s), 0)),
        ],
        out_shape=[y_sds, y_sds],
        scratch_shapes=[pltpu.VMEM((2, SSD_HEADS // 2, 2 * SSD_HEAD_DIM, SSD_STATE), F32)],
        compiler_params=_cparams(("arbitrary",)),
        name="ssd_scan",
    )(xa, xa, p, p, bias, alog)


def _ssd_out_kernel(yf_ref, yb_ref, xs_ref, z_ref, d_ref, g_ref, o_ref):
    y = yf_ref[...] + yb_ref[...] + d_ref[...] * xs_ref[...]
    y = y * _silu(z_ref[...])
    o_ref[...] = _rms(y, g_ref[...]).astype(o_ref.dtype)


def _ssd_out(y_f, y_b, xa, p, d_skip, norm_g):
    n = xa.shape[0]
    t = ROW_TILE
    vec = pl.BlockSpec((1, SSD_WIDTH), lambda i: (0, 0))
    row = pl.BlockSpec((t, SSD_WIDTH), lambda i: (i, 0))
    return pl.pallas_call(
        _ssd_out_kernel,
        grid=(n // t,),
        in_specs=[
            row, row, row,
            pl.BlockSpec((t, SSD_WIDTH), lambda i: (i, COL_Z // SSD_WIDTH)),
            vec, vec,
        ],
        out_specs=pl.BlockSpec((t, SSD_WIDTH), lambda i: (i, 0)),
        out_shape=jax.ShapeDtypeStruct((n, SSD_WIDTH), BF16),
        compiler_params=_cparams(("parallel",)),
        name="ssd_out",
    )(y_f, y_b, xa, p, jnp.repeat(d_skip, SSD_HEAD_DIM).reshape(1, SSD_WIDTH), norm_g.reshape(1, SSD_WIDTH))


def _out_proj_kernel(x_ref, a_ref, s_ref, c_ref, mod_ref, w_ref, o_ref, *, n_lat, tm):
    i = pl.program_id(0)
    acc = jnp.dot(a_ref[...], w_ref[0:ATTN_WIDTH, :], preferred_element_type=F32)
    acc += jnp.dot(s_ref[...], w_ref[ATTN_WIDTH:ATTN_WIDTH + SSD_WIDTH, :], preferred_element_type=F32)
    acc += jnp.dot(c_ref[...], w_ref[ATTN_WIDTH + SSD_WIDTH:, :], preferred_element_type=F32)
    for r in range(0, tm, ROW_CHUNK):
        rs = slice(r, r + ROW_CHUNK)
        o_ref[rs, :] = x_ref[rs, :] + _mod_row(mod_ref, 2, i * tm + r, n_lat) * acc[rs, :]


def _out_proj(xs, attn, ssd, cm, mod3, w_out, *, rows, tm, n_lat):
    d = xs.shape[1]
    return pl.pallas_call(
        functools.partial(_out_proj_kernel, n_lat=n_lat, tm=tm),
        grid=(rows // tm,),
        in_specs=[
            pl.BlockSpec((tm, d), lambda i: (i, 0)),
            pl.BlockSpec((tm, ATTN_WIDTH), lambda i: (i, 0)),
            pl.BlockSpec((tm, SSD_WIDTH), lambda i: (i, 0)),
            pl.BlockSpec((tm, CM_WIDTH), lambda i: (i, 0)),
            pl.BlockSpec((3, SUBLANES, d), lambda i: (0, 0, 0)),
            pl.BlockSpec((D_MIX, d), lambda i: (0, 0)),
        ],
        out_specs=pl.BlockSpec((tm, d), lambda i: (i, 0)),
        out_shape=jax.ShapeDtypeStruct((rows, d), F32),
        compiler_params=_cparams(("parallel",)),
        name="out_proj",
    )(xs, attn, ssd, cm, mod3, w_out)


def _rope_tables(n_lat, n):
    rows = n_lat // GRID_W
    row = jnp.repeat(jnp.arange(rows), GRID_W).astype(F32)
    col = jnp.tile(jnp.arange(GRID_W), rows).astype(F32)
    inv = ROPE_THETA ** (-jnp.arange(0, ROPE_AXIS_DIM, 2, dtype=F32) / ROPE_AXIS_DIM)
    ar = row[:, None] * inv
    ac = col[:, None] * inv
    cos = jnp.concatenate([jnp.cos(ar), jnp.cos(ar), jnp.cos(ac), jnp.cos(ac)], axis=1)
    sin = jnp.concatenate([-jnp.sin(ar), jnp.sin(ar), -jnp.sin(ac), jnp.sin(ac)], axis=1)
    pad = n - n_lat
    cos = jnp.concatenate([cos, jnp.ones((pad, HEAD_DIM), F32)], axis=0)
    sin = jnp.concatenate([sin, jnp.zeros((pad, HEAD_DIM), F32)], axis=0)
    return cos, sin


def _relayout_w_in(w_in):
    d = w_in.shape[0]
    o_dt = COL_GLU
    main = w_in[:, :o_dt]
    dt = w_in[:, o_dt:o_dt + 2 * SSD_HEADS]
    glu = w_in[:, o_dt + 2 * SSD_HEADS:]
    zpad = jnp.zeros((d, LANES - SSD_HEADS), w_in.dtype)
    return jnp.concatenate([main, glu, dt[:, :SSD_HEADS], zpad, dt[:, SSD_HEADS:], zpad], axis=1).astype(BF16)


def kernel(x, c, ctx, c_ctx, w_mod, b_mod, norm_g, w_ffn_in, w_ffn_out, w_in, w_out, qk_g,
           ssd_conv_w, ssd_conv_b, ssd_dt_bias, ssd_a_log, ssd_d, ssd_norm_g,
           cm_dw_w, cm_dw_b, cm_ln_g, cm_ln_b, final_g):
    assert x.shape[0] == 1 and ctx.shape[0] == 1
    n_lat, d = x.shape[1], x.shape[2]
    n_ctx = ctx.shape[1]
    n = n_lat + n_ctx
    depth = w_mod.shape[0]
    assert n % MM_TILE == 0 and n_lat % ROW_TILE == 0 and n_ctx % ROW_TILE == 0
    assert ROW_TILE % ROW_CHUNK == 0 and MM_TILE % (2 * ROW_CHUNK) == 0
    kv_tile = max(t for t in range(KEY_CHUNK, KV_TILE + 1, KEY_CHUNK) if n % t == 0)
    lat_tm = 512
    assert n_lat % lat_tm == 0 and n_lat % GRID_W == 0

    xs = jnp.concatenate([x[0], ctx[0]], axis=0)
    cvec = jnp.concatenate([c_ctx[None, :], c, jnp.zeros((SUBLANES - 2, d), F32)], axis=0)
    mods = _modulation(cvec, w_mod, b_mod)
    mods = mods.reshape(depth, SUBLANES, N_MOD, d).transpose(0, 2, 1, 3)
    cos, sin = _rope_tables(n_lat, n)
    w_ffn_in_b = w_ffn_in.astype(BF16)
    w_ffn_out_b = w_ffn_out.astype(BF16)
    w_out_b = w_out.astype(BF16)

    out = None
    for l in range(depth):
        last = l == depth - 1
        m = mods[l]
        xs = _ffn(xs, m[0:3], norm_g[l, 0], w_ffn_in_b, w_ffn_out_b, l, 0,
                  rows=n, tm=MM_TILE, n_lat=n_lat)
        p = _in_proj(xs, m[3:6], norm_g[l, 1], _relayout_w_in(w_in[l]), tm=MM_TILE, n_lat=n_lat)
        q, k, v = _qkv_prep(p, cos, sin, qk_g[l])
        attn = _attention(q, k, v, q_tiles=n_lat // Q_TILE, q_off=0, kv_tiles=n // kv_tile, kv_off=0,
                          tq=Q_TILE, tk=kv_tile)
        if not last:
            attn = _attention(q, k, v, q_tiles=n_ctx // ROW_TILE, q_off=n_lat // ROW_TILE,
                              kv_tiles=n_ctx // ROW_TILE, kv_off=n_lat // ROW_TILE,
                              tq=ROW_TILE, tk=ROW_TILE, prev=attn)
        xa = _ssd_conv(p, ssd_conv_w[l], ssd_conv_b[l], n_lat=n_lat)
        y_f, y_b = _ssd_scan(xa, p, ssd_dt_bias[l], ssd_a_log[l], n_lat=n_lat)
        ssd = _ssd_out(y_f, y_b, xa, p, ssd_d[l], ssd_norm_g[l])
        cm = _conv_module(p, cm_dw_w[l], cm_dw_b[l], cm_ln_g[l], cm_ln_b[l], n_lat=n_lat)
        if last:
            xs = _out_proj(xs, attn, ssd, cm, m[3:6], w_out_b[l], rows=n_lat, tm=lat_tm // 2, n_lat=n_lat)
            out = _ffn(xs, m[6:9], norm_g[l, 2], w_ffn_in_b, w_ffn_out_b, l, 1,
                       rows=n_lat, tm=lat_tm, n_lat=n_lat, final_g=final_g)
        else:
            xs = _out_proj(xs, attn, ssd, cm, m[3:6], w_out_b[l], rows=n, tm=MM_TILE // 2, n_lat=n_lat)
            xs = _ffn(xs, m[6:9], norm_g[l, 2], w_ffn_in_b, w_ffn_out_b, l, 1,
                      rows=n, tm=MM_TILE, n_lat=n_lat)
    return out[None]
```

```python
import functools
import math

import jax
import jax.numpy as jnp
from jax import lax
from jax.experimental import pallas as pl
from jax.experimental.pallas import tpu as pltpu

F32 = jnp.float32
BF16 = jnp.bfloat16

D_MODEL = 2048
DEPTH = 2
GRID_W = 64
N_MOD = 9
D_FF = 5632
EPS = 1e-6

HEAD_DIM = 128
ATTN_HEADS = 8
ATTN_KV_HEADS = 2
ATTN_GROUPS = ATTN_HEADS // ATTN_KV_HEADS
ATTN_WIDTH = ATTN_HEADS * HEAD_DIM
KV_WIDTH = ATTN_KV_HEADS * HEAD_DIM
ROPE_THETA = 10000.0
ROPE_AXIS_DIM = HEAD_DIM // 2

SSD_HEADS = 8
SSD_HEAD_DIM = 64
SSD_WIDTH = SSD_HEADS * SSD_HEAD_DIM
SSD_GROUPS = 2
SSD_STATE = 128
SSD_CONV = 7
SSD_CHUNK = 128
SSD_CONV_CH = SSD_WIDTH + 2 * SSD_GROUPS * SSD_STATE

CM_WIDTH = 512
CM_KERNEL = 31
D_MIX = ATTN_WIDTH + SSD_WIDTH + CM_WIDTH

LANES = 128
SUBLANES = 8
VMEM_LIMIT = 56 * 1024 * 1024

QKV_COLS = ATTN_WIDTH + 2 * KV_WIDTH
COL_XBC = 0
COL_Z = COL_XBC + SSD_CONV_CH
COL_GLU = COL_Z + SSD_WIDTH
COL_DT = COL_GLU + 2 * CM_WIDTH
P_COLS = 2 * QKV_COLS

ROW_TILE = 256
MM_TILE = 640
FF_CHUNK = 512
KV_TILE = 3328
Q_TILE = 256
EXP_LANES = 1024
SSD_HALO = 8
CM_HALO = 16
BF16_ROWS = 16
VT_ROWS = HEAD_DIM + BF16_ROWS
QK_EXP2_SCALE = (HEAD_DIM ** -0.5) * math.log2(math.e)
KEY_CHUNK = 256
SCORE_BUFFERS = 3
SCORE_AHEAD = 1


def _cparams(sem):
    return pltpu.CompilerParams(dimension_semantics=sem, vmem_limit_bytes=VMEM_LIMIT)


def _sigmoid(x):
    return 1.0 / (1.0 + jnp.exp(-x))


def _silu(x):
    return x * _sigmoid(x)


def _rms(x, g):
    return x * lax.rsqrt(jnp.mean(x * x, axis=-1, keepdims=True) + EPS) * g


ROW_CHUNK = 64


def _mod_row(mod_ref, k, row0, n_lat):
    cls = (row0 < n_lat).astype(jnp.int32)
    return mod_ref[k, pl.ds(cls, 1), :]


def _norm_mod_store(h_scr, x_ref, mod_ref, g_ref, row0, tm, n_lat):
    for r in range(0, tm, ROW_CHUNK):
        shift = _mod_row(mod_ref, 0, row0 + r, n_lat)
        gain = g_ref[...] * (1.0 + _mod_row(mod_ref, 1, row0 + r, n_lat))
        x = x_ref[r:r + ROW_CHUNK, :]
        inv = lax.rsqrt(jnp.mean(x * x, axis=-1, keepdims=True) + EPS)
        h_scr[r:r + ROW_CHUNK, :] = ((x * inv) * gain + shift).astype(BF16)


def _mod_kernel(c_ref, w_ref, b_ref, o_ref):
    a = _silu(c_ref[...]).astype(BF16)
    o_ref[0] = jnp.dot(a, w_ref[0].astype(BF16), preferred_element_type=F32) + b_ref[0]


def _modulation(cvec, w_mod, b_mod):
    depth, d, n = w_mod.shape
    tn = 1024
    return pl.pallas_call(
        _mod_kernel,
        grid=(depth, n // tn),
        in_specs=[
            pl.BlockSpec((SUBLANES, d), lambda l, j: (0, 0)),
            pl.BlockSpec((1, d, tn), lambda l, j: (l, 0, j)),
            pl.BlockSpec((1, 1, tn), lambda l, j: (l, 0, j)),
        ],
        out_specs=pl.BlockSpec((1, SUBLANES, tn), lambda l, j: (l, 0, j)),
        out_shape=jax.ShapeDtypeStruct((depth, SUBLANES, n), F32),
        compiler_params=_cparams(("arbitrary", "arbitrary")),
        name="modulation",
    )(cvec, w_mod, b_mod.reshape(depth, 1, n))


def _ffn_kernel(x_ref, mod_ref, g_ref, wg_ref, wu_ref, wo_ref, *rest, n_lat, tm, final):
    if final:
        fg_ref, o_ref, h_scr, acc_scr = rest
    else:
        o_ref, h_scr, acc_scr = rest
    i = pl.program_id(0)
    f = pl.program_id(1)

    @pl.when(f == 0)
    def _():
        _norm_mod_store(h_scr, x_ref, mod_ref, g_ref, i * tm, tm, n_lat)
        acc_scr[...] = jnp.zeros_like(acc_scr)

    h = h_scr[...]
    gt = jnp.dot(h, wg_ref[...], preferred_element_type=F32)
    up = jnp.dot(h, wu_ref[...], preferred_element_type=F32)
    a = (_silu(gt) * up).astype(BF16)
    acc_scr[...] += jnp.dot(a, wo_ref[...], preferred_element_type=F32)

    @pl.when(f == pl.num_programs(1) - 1)
    def _():
        for r in range(0, tm, ROW_CHUNK):
            rs = slice(r, r + ROW_CHUNK)
            gate = _mod_row(mod_ref, 2, i * tm + r, n_lat)
            y = x_ref[rs, :] + (0.5 * gate) * acc_scr[rs, :]
            if final:
                y = _rms(y, fg_ref[...])
            o_ref[rs, :] = y


def _ffn(xs, mod3, norm_g, w_in, w_out, layer, which, *, rows, tm, n_lat, final_g=None):
    d = xs.shape[1]
    nf = D_FF // FF_CHUNK
    final = final_g is not None
    in_specs = [
        pl.BlockSpec((tm, d), lambda i, f: (i, 0)),
        pl.BlockSpec((3, SUBLANES, d), lambda i, f: (0, 0, 0)),
        pl.BlockSpec((1, d), lambda i, f: (0, 0)),
        pl.BlockSpec((None, None, d, FF_CHUNK), lambda i, f: (layer, which, 0, f)),
        pl.BlockSpec((None, None, d, FF_CHUNK), lambda i, f: (layer, which, 0, nf + f)),
        pl.BlockSpec((None, None, FF_CHUNK, d), lambda i, f: (layer, which, f, 0)),
    ]
    args = [xs, mod3, norm_g.reshape(1, d), w_in, w_in, w_out]
    if final:
        in_specs.append(pl.BlockSpec((1, d), lambda i, f: (0, 0)))
        args.append(final_g.reshape(1, d))
    return pl.pallas_call(
        functools.partial(_ffn_kernel, n_lat=n_lat, tm=tm, final=final),
        grid=(rows // tm, nf),
        in_specs=in_specs,
        out_specs=pl.BlockSpec((tm, d), lambda i, f: (i, 0)),
        out_shape=jax.ShapeDtypeStruct((rows, d), F32),
        scratch_shapes=[pltpu.VMEM((tm, d), BF16), pltpu.VMEM((tm, d), F32)],
        compiler_params=_cparams(("parallel", "arbitrary")),
        name="ffn",
    )(*args)


def _proj_kernel(x_ref, mod_ref, g_ref, w_ref, cos_ref, sin_ref, qkg_ref, qo_ref, ko_ref, vo_ref, po_ref,
                 h_scr, qkv_scr, *, n_lat, tm):
    i = pl.program_id(0)
    j = pl.program_id(1)

    @pl.when(j == 0)
    def _():
        _norm_mod_store(h_scr, x_ref, mod_ref, g_ref, i * tm, tm, n_lat)
        qkv_scr[...] = jnp.dot(h_scr[...], w_ref[...], preferred_element_type=F32)
        cos = cos_ref[...]
        sin = sin_ref[...]

        def norm_rope(xh, g):
            xh = _rms(xh, g)
            return xh * cos + pltpu.roll(xh, HEAD_DIM // 2, 1) * sin

        for h in range(ATTN_HEADS):
            sl = slice(h * HEAD_DIM, (h + 1) * HEAD_DIM)
            qo_ref[:, sl] = (norm_rope(qkv_scr[:, sl], qkg_ref[0:1, :]) * QK_EXP2_SCALE).astype(BF16)
        for h in range(ATTN_KV_HEADS):
            sl = slice(ATTN_WIDTH + h * HEAD_DIM, ATTN_WIDTH + (h + 1) * HEAD_DIM)
            ko_ref[:, h * HEAD_DIM:(h + 1) * HEAD_DIM] = norm_rope(qkv_scr[:, sl], qkg_ref[1:2, :]).astype(BF16)
        vt = qkv_scr[:, ATTN_WIDTH + KV_WIDTH:].T.astype(BF16)
        ones = jnp.ones((VT_ROWS - HEAD_DIM, tm), BF16)
        for g in range(ATTN_KV_HEADS):
            vo_ref[g * VT_ROWS:g * VT_ROWS + HEAD_DIM, :] = vt[g * HEAD_DIM:(g + 1) * HEAD_DIM, :]
            vo_ref[g * VT_ROWS + HEAD_DIM:(g + 1) * VT_ROWS, :] = ones

    @pl.when(j > 0)
    def _():
        po_ref[...] = jnp.dot(h_scr[...], w_ref[...], preferred_element_type=F32)


def _in_proj(xs, mod3, norm_g, w_in_r, cos, sin, qk_g, *, tm, n_lat):
    n, d = xs.shape
    tn = QKV_COLS
    steps = (QKV_COLS + P_COLS) // tn
    return pl.pallas_call(
        functools.partial(_proj_kernel, n_lat=n_lat, tm=tm),
        grid=(n // tm, steps),
        in_specs=[
            pl.BlockSpec((tm, d), lambda i, j: (i, 0)),
            pl.BlockSpec((3, SUBLANES, d), lambda i, j: (0, 0, 0)),
            pl.BlockSpec((1, d), lambda i, j: (0, 0)),
            pl.BlockSpec((d, tn), lambda i, j: (0, j)),
            pl.BlockSpec((tm, HEAD_DIM), lambda i, j: (i, 0)),
            pl.BlockSpec((tm, HEAD_DIM), lambda i, j: (i, 0)),
            pl.BlockSpec((2, HEAD_DIM), lambda i, j: (0, 0)),
        ],
        out_specs=[
            pl.BlockSpec((tm, ATTN_WIDTH), lambda i, j: (i, 0)),
            pl.BlockSpec((tm, KV_WIDTH), lambda i, j: (i, 0)),
            pl.BlockSpec((ATTN_KV_HEADS * VT_ROWS, tm), lambda i, j: (0, i)),
            pl.BlockSpec((tm, tn), lambda i, j: (i, jnp.maximum(j - 1, 0))),
        ],
        out_shape=[
            jax.ShapeDtypeStruct((n, ATTN_WIDTH), BF16),
            jax.ShapeDtypeStruct((n, KV_WIDTH), BF16),
            jax.ShapeDtypeStruct((ATTN_KV_HEADS * VT_ROWS, n), BF16),
            jax.ShapeDtypeStruct((n, P_COLS), F32),
        ],
        scratch_shapes=[pltpu.VMEM((tm, d), BF16), pltpu.VMEM((tm, tn), F32)],
        compiler_params=_cparams(("parallel", "arbitrary")),
        name="in_proj",
    )(xs, mod3, norm_g.reshape(1, d), w_in_r, cos, sin, qk_g)


def _flash_kernel(*refs, aliased):
    if aliased:
        q_ref, k_ref, vt_ref, _, o_ref, m_scr, acc_scr, *bufs = refs
    else:
        q_ref, k_ref, vt_ref, o_ref, m_scr, acc_scr, *bufs = refs
    s_bufs, p_bufs = bufs[:len(bufs) // 2], bufs[len(bufs) // 2:]
    j = pl.program_id(1)
    tq = q_ref.shape[0]
    tk = k_ref.shape[0]
    w = ATTN_GROUPS * tq
    kc = min(KEY_CHUNK, tk)

    @pl.when(j == 0)
    def _():
        m_scr[...] = jnp.full_like(m_scr, -jnp.inf)
        acc_scr[...] = jnp.zeros_like(acc_scr)

    qs = [jnp.concatenate(
        [q_ref[:, (ATTN_GROUPS * g + h) * HEAD_DIM:(ATTN_GROUPS * g + h + 1) * HEAD_DIM]
         for h in range(ATTN_GROUPS)], axis=0) for g in range(ATTN_KV_HEADS)]
    units = [(g, ci) for ci in range(tk // kc) for g in range(ATTN_KV_HEADS)]

    def scores(u):
        g, ci = units[u]
        st = lax.dot_general(k_ref[ci * kc:(ci + 1) * kc, g * HEAD_DIM:(g + 1) * HEAD_DIM], qs[g],
                             (((1,), (1,)), ((), ())), preferred_element_type=F32)
        s_bufs[u % len(s_bufs)][...] = st
        mx = st[0:SUBLANES]
        for r in range(1, kc // SUBLANES):
            mx = jnp.maximum(mx, st[r * SUBLANES:(r + 1) * SUBLANES])
        return mx

    m_run = [m_scr[g] for g in range(ATTN_KV_HEADS)]
    row0 = jnp.minimum(j, 0)
    pending = [scores(u) for u in range(min(SCORE_AHEAD, len(units)))]
    for u, (g, ci) in enumerate(units):
        mx = pending.pop(0)
        if u + SCORE_AHEAD < len(units):
            pending.append(scores(u + SCORE_AHEAD))
        m_new = jnp.maximum(m_run[g], jnp.max(mx, axis=0, keepdims=True))
        alpha = jnp.exp2(m_run[g] - m_new)
        s_buf, p_buf = s_bufs[u % len(s_bufs)], p_bufs[u % len(p_bufs)]
        for c0 in range(0, w, EXP_LANES):
            cs = slice(c0, min(c0 + EXP_LANES, w))
            m_rep = jnp.broadcast_to(m_new[:, cs], (BF16_ROWS, cs.stop - cs.start))
            for r in range(kc // BF16_ROWS):
                rs = slice(r * BF16_ROWS, (r + 1) * BF16_ROWS)
                rd = pl.ds(pl.multiple_of(row0 + r * BF16_ROWS, BF16_ROWS), BF16_ROWS)
                p_buf[rs, cs] = jnp.exp2(s_buf[rd, cs] - m_rep).astype(BF16)
        pv = jnp.dot(vt_ref[g * VT_ROWS:(g + 1) * VT_ROWS, ci * kc:(ci + 1) * kc], p_buf[...],
                     preferred_element_type=F32)
        acc_scr[g] = alpha * acc_scr[g] + pv
        m_run[g] = m_new
    for g in range(ATTN_KV_HEADS):
        m_scr[g] = m_run[g]

    @pl.when(j == pl.num_programs(1) - 1)
    def _():
        for g in range(ATTN_KV_HEADS):
            acc = acc_scr[g]
            ot = acc[0:HEAD_DIM] / acc[HEAD_DIM:HEAD_DIM + 1]
            for h in range(ATTN_GROUPS):
                col = (ATTN_GROUPS * g + h) * HEAD_DIM
                o_ref[:, col:col + HEAD_DIM] = ot[:, h * tq:(h + 1) * tq].T.astype(o_ref.dtype)


def _attention(q, k, vt, *, q_tiles, q_off, kv_tiles, kv_off, tq, tk, prev=None):
    n = q.shape[0]
    aliased = prev is not None
    in_specs = [
        pl.BlockSpec((tq, ATTN_WIDTH), lambda i, j: (q_off + i, 0)),
        pl.BlockSpec((tk, KV_WIDTH), lambda i, j: (kv_off + j, 0)),
        pl.BlockSpec((ATTN_KV_HEADS * VT_ROWS, tk), lambda i, j: (0, kv_off + j)),
    ]
    args = [q, k, vt]
    if aliased:
        in_specs.append(pl.BlockSpec(memory_space=pl.ANY))
        args.append(prev)
    w = ATTN_GROUPS * tq
    kc = min(KEY_CHUNK, tk)
    units = min(ATTN_KV_HEADS * (tk // kc), SCORE_BUFFERS)
    return pl.pallas_call(
        functools.partial(_flash_kernel, aliased=aliased),
        grid=(q_tiles, kv_tiles),
        in_specs=in_specs,
        out_specs=pl.BlockSpec((tq, ATTN_WIDTH), lambda i, j: (q_off + i, 0)),
        out_shape=jax.ShapeDtypeStruct((n, ATTN_WIDTH), BF16),
        scratch_shapes=[
            pltpu.VMEM((ATTN_KV_HEADS, 1, w), F32),
            pltpu.VMEM((ATTN_KV_HEADS, VT_ROWS, w), F32),
        ] + [pltpu.VMEM((kc, w), F32)] * units + [pltpu.VMEM((kc, w), BF16)] * units,
        input_output_aliases={3: 0} if aliased else {},
        compiler_params=_cparams(("parallel", "arbitrary")),
        name="attention_ctx" if aliased else "attention",
    )(*args)


def _segment_edges(i, lat_tiles):
    first = jnp.logical_or(i == 0, i == lat_tiles)
    last = jnp.logical_or(i == lat_tiles - 1, i == pl.num_programs(0) - 1)
    return first, last


def _ssd_conv_kernel(prev_ref, cur_ref, next_ref, w_ref, b_ref, o_ref, ext_scr, *, lat_tiles):
    i = pl.program_id(0)
    t = cur_ref.shape[0]
    first, last = _segment_edges(i, lat_tiles)
    ext_scr[0:SSD_HALO, :] = jnp.where(first, 0.0, prev_ref[...])
    ext_scr[SSD_HALO:SSD_HALO + t, :] = cur_ref[...]
    ext_scr[SSD_HALO + t:, :] = jnp.where(last, 0.0, next_ref[...])
    pad = SSD_CONV // 2
    acc = jnp.broadcast_to(b_ref[...], cur_ref.shape)
    for k in range(SSD_CONV):
        acc = acc + w_ref[k:k + 1, :] * ext_scr[pl.ds(SSD_HALO - pad + k, t), :]
    o_ref[...] = _silu(acc)


def _ssd_conv(p, conv_w, conv_b, *, n_lat):
    n = p.shape[0]
    t = ROW_TILE
    hb = t // SSD_HALO
    last_hb = n // SSD_HALO - 1
    cb = COL_XBC // SSD_CONV_CH
    return pl.pallas_call(
        functools.partial(_ssd_conv_kernel, lat_tiles=n_lat // t),
        grid=(n // t,),
        in_specs=[
            pl.BlockSpec((SSD_HALO, SSD_CONV_CH), lambda i: (jnp.maximum(i * hb - 1, 0), cb)),
            pl.BlockSpec((t, SSD_CONV_CH), lambda i: (i, cb)),
            pl.BlockSpec((SSD_HALO, SSD_CONV_CH), lambda i: (jnp.minimum((i + 1) * hb, last_hb), cb)),
            pl.BlockSpec((SSD_CONV, SSD_CONV_CH), lambda i: (0, 0)),
            pl.BlockSpec((1, SSD_CONV_CH), lambda i: (0, 0)),
        ],
        out_specs=pl.BlockSpec((t, SSD_CONV_CH), lambda i: (i, 0)),
        out_shape=jax.ShapeDtypeStruct((n, SSD_CONV_CH), F32),
        scratch_shapes=[pltpu.VMEM((t + 2 * SSD_HALO, SSD_CONV_CH), F32)],
        compiler_params=_cparams(("parallel",)),
        name="ssd_conv",
    )(p, p, p, conv_w, conv_b.reshape(1, SSD_CONV_CH))


def _cm_kernel(ap_ref, gp_ref, a_ref, g_ref, an_ref, gn_ref, w_ref, b_ref, lg_ref, lb_ref, o_ref, ext_scr,
               sh_scr, *, lat_tiles):
    i = pl.program_id(0)
    t = a_ref.shape[0]
    first, last = _segment_edges(i, lat_tiles)

    def glu(a, gt):
        return a[...] * _sigmoid(gt[...])

    ext_scr[0:CM_HALO, :] = jnp.where(first, 0.0, glu(ap_ref, gp_ref))
    ext_scr[CM_HALO:CM_HALO + t, :] = glu(a_ref, g_ref)
    ext_scr[CM_HALO + t:, :] = jnp.where(last, 0.0, glu(an_ref, gn_ref))
    pad = CM_KERNEL // 2
    first_off = CM_HALO - pad
    span = t + (first_off + CM_KERNEL - 1) // SUBLANES * SUBLANES
    acc = jnp.broadcast_to(b_ref[...], a_ref.shape)
    for phase in range(SUBLANES):
        taps = [k for k in range(CM_KERNEL) if (first_off + k) % SUBLANES == phase]
        if not taps:
            continue
        shifted = sh_scr.at[phase % 2]
        shifted[...] = ext_scr[phase:phase + span, :]
        for k in taps:
            row = first_off + k - phase
            acc = acc + w_ref[k:k + 1, :] * shifted[row:row + t, :]
    mu = jnp.mean(acc, axis=-1, keepdims=True)
    cen = acc - mu
    var = jnp.mean(cen * cen, axis=-1, keepdims=True)
    y = cen * lax.rsqrt(var + EPS) * lg_ref[...] + lb_ref[...]
    o_ref[...] = _silu(y).astype(o_ref.dtype)


def _conv_module(p, dw_w, dw_b, ln_g, ln_b, *, n_lat):
    n = p.shape[0]
    t = ROW_TILE
    hb = t // CM_HALO
    last_hb = n // CM_HALO - 1
    ca = COL_GLU // CM_WIDTH
    cg = ca + 1

    def prev_map(c):
        return lambda i: (jnp.maximum(i * hb - 1, 0), c)

    def next_map(c):
        return lambda i: (jnp.minimum((i + 1) * hb, last_hb), c)

    vec = pl.BlockSpec((1, CM_WIDTH), lambda i: (0, 0))
    return pl.pallas_call(
        functools.partial(_cm_kernel, lat_tiles=n_lat // t),
        grid=(n // t,),
        in_specs=[
            pl.BlockSpec((CM_HALO, CM_WIDTH), prev_map(ca)),
            pl.BlockSpec((CM_HALO, CM_WIDTH), prev_map(cg)),
            pl.BlockSpec((t, CM_WIDTH), lambda i: (i, ca)),
            pl.BlockSpec((t, CM_WIDTH), lambda i: (i, cg)),
            pl.BlockSpec((CM_HALO, CM_WIDTH), next_map(ca)),
            pl.BlockSpec((CM_HALO, CM_WIDTH), next_map(cg)),
            pl.BlockSpec((CM_KERNEL, CM_WIDTH), lambda i: (0, 0)),
            vec, vec, vec,
        ],
        out_specs=pl.BlockSpec((t, CM_WIDTH), lambda i: (i, 0)),
        out_shape=jax.ShapeDtypeStruct((n, CM_WIDTH), BF16),
        scratch_shapes=[pltpu.VMEM((t + 2 * CM_HALO, CM_WIDTH), F32),
                        pltpu.VMEM((2, t + 2 * CM_HALO - SUBLANES, CM_WIDTH), F32)],
        compiler_params=_cparams(("parallel",)),
        name="conv_module",
    )(p, p, p, p, p, p, dw_w, dw_b.reshape(1, -1), ln_g.reshape(1, -1), ln_b.reshape(1, -1))


def _split3_dot(tri, x):
    hi = x.astype(BF16)
    r1 = x - hi.astype(F32)
    mid = r1.astype(BF16)
    lo = (r1 - mid.astype(F32)).astype(BF16)
    return (jnp.dot(tri, hi, preferred_element_type=F32) + jnp.dot(tri, mid, preferred_element_type=F32)
            + jnp.dot(tri, lo, preferred_element_type=F32))


def _ssd_chunk(xa_ref, dt_ref, bias, alog, y_ref, h_scr, backward):
    q = SSD_CHUNK
    x = dt_ref[...] + bias
    dt = jnp.maximum(x, 0.0) + jnp.log1p(jnp.exp(-jnp.abs(x)))
    da = dt * (-jnp.exp(alog))
    ii = lax.broadcasted_iota(jnp.int32, (q, q), 0)
    jj = lax.broadcasted_iota(jnp.int32, (q, q), 1)
    lane = jj
    inc = (jj >= ii) if backward else (jj <= ii)
    tri = jnp.where(inc, 1.0, 0.0).astype(BF16)
    cum = _split3_dot(tri, da)
    tot = jnp.sum(da, axis=0, keepdims=True)
    cum_t = cum.T
    dt_t = dt.T
    e_in = jnp.exp(cum)
    w_out = jnp.exp(tot - cum) * dt
    e_tot = jnp.exp(tot)

    xa = xa_ref[...]
    pair = 2 * SSD_HEAD_DIM
    low = lane < SSD_HEAD_DIM
    for g in range(SSD_GROUPS):
        b = xa[:, SSD_WIDTH + g * SSD_STATE:SSD_WIDTH + (g + 1) * SSD_STATE].astype(BF16)
        cg = xa[:, SSD_WIDTH + (SSD_GROUPS + g) * SSD_STATE:
                SSD_WIDTH + (SSD_GROUPS + g + 1) * SSD_STATE].astype(BF16)
        cb = lax.dot_general(cg, b, (((1,), (1,)), ((), ())), preferred_element_type=F32)
        for m in range(SSD_HEADS // SSD_GROUPS // 2):
            pi = g * (SSD_HEADS // SSD_GROUPS // 2) + m
            h0, h1 = 2 * pi, 2 * pi + 1
            xp = xa[:, pi * pair:(pi + 1) * pair]
            xp_b = xp.astype(BF16)
            ys = []
            for hd in (h0, h1):
                seg = cum[:, hd:hd + 1] - cum_t[hd:hd + 1, :]
                dec = jnp.exp(jnp.where(inc, seg, -jnp.inf))
                sc = (cb * dec * dt_t[hd:hd + 1, :]).astype(BF16)
                ys.append(jnp.dot(sc, xp_b, preferred_element_type=F32))
            y_diag = jnp.where(low, ys[0], ys[1])
            hp = h_scr[pi]
            y_off = lax.dot_general(cg, hp.astype(BF16), (((1,), (1,)), ((), ())),
                                    preferred_element_type=F32)
            y_off = y_off * jnp.where(low, e_in[:, h0:h0 + 1], e_in[:, h1:h1 + 1])
            y_ref[:, pi * pair:(pi + 1) * pair] = y_diag + y_off
            xw = xp * jnp.where(low, w_out[:, h0:h0 + 1], w_out[:, h1:h1 + 1])
            st = jnp.dot(xw.T.astype(BF16), b, preferred_element_type=F32)
            row_low = ii < SSD_HEAD_DIM
            keep = jnp.where(row_low, e_tot[:, h0:h0 + 1], e_tot[:, h1:h1 + 1])
            h_scr[pi] = keep * hp + st


def _ssd_scan_kernel(xaf_ref, xab_ref, dtf_ref, dtb_ref, bias_ref, alog_ref, yf_ref, yb_ref, h_scr):
    @pl.when(pl.program_id(0) == 0)
    def _():
        h_scr[...] = jnp.zeros_like(h_scr)

    _ssd_chunk(xaf_ref, dtf_ref, bias_ref[0], alog_ref[0], yf_ref, h_scr.at[0], False)
    _ssd_chunk(xab_ref, dtb_ref, bias_ref[1], alog_ref[1], yb_ref, h_scr.at[1], True)


def _ssd_scan(xa, p, dt_bias, a_log, *, n_lat):
    n = xa.shape[0]
    nc = n // SSD_CHUNK
    lat_c = n_lat // SSD_CHUNK
    ctx_c = nc - lat_c
    dt_cb = COL_DT // LANES

    def fwd(s):
        return jnp.where(s < ctx_c, lat_c + s, s - ctx_c)

    def bwd(s):
        return nc - 1 - s

    pad = LANES - SSD_HEADS
    bias = jnp.pad(dt_bias, ((0, 0), (0, pad))).reshape(2, 1, LANES)
    alog = jnp.pad(a_log, ((0, 0), (0, pad))).reshape(2, 1, LANES)
    y_sds = jax.ShapeDtypeStruct((n, SSD_WIDTH), F32)
    return pl.pallas_call(
        _ssd_scan_kernel,
        grid=(nc,),
        in_specs=[
            pl.BlockSpec((SSD_CHUNK, SSD_CONV_CH), lambda s: (fwd(s), 0)),
            pl.BlockSpec((SSD_CHUNK, SSD_CONV_CH), lambda s: (bwd(s), 0)),
            pl.BlockSpec((SSD_CHUNK, LANES), lambda s: (fwd(s), dt_cb)),
            pl.BlockSpec((SSD_CHUNK, LANES), lambda s: (bwd(s), dt_cb + 1)),
            pl.BlockSpec((2, 1, LANES), lambda s: (0, 0, 0)),
            pl.BlockSpec((2, 1, LANES), lambda s: (0, 0, 0)),
        ],
        out_specs=[
            pl.BlockSpec((SSD_CHUNK, SSD_WIDTH), lambda s: (fwd(s), 0)),
            pl.BlockSpec((SSD_CHUNK, SSD_WIDTH), lambda s: (bwd(s), 0)),
        ],
        out_shape=[y_sds, y_sds],
        scratch_shapes=[pltpu.VMEM((2, SSD_HEADS // 2, 2 * SSD_HEAD_DIM, SSD_STATE), F32)],
        compiler_params=_cparams(("arbitrary",)),
        name="ssd_scan",
    )(xa, xa, p, p, bias, alog)


def _ssd_out_kernel(yf_ref, yb_ref, xs_ref, z_ref, d_ref, g_ref, o_ref):
    y = yf_ref[...] + yb_ref[...] + d_ref[...] * xs_ref[...]
    y = y * _silu(z_ref[...])
    o_ref[...] = _rms(y, g_ref[...]).astype(o_ref.dtype)


def _ssd_out(y_f, y_b, xa, p, d_skip, norm_g):
    n = xa.shape[0]
    t = ROW_TILE
    vec = pl.BlockSpec((1, SSD_WIDTH), lambda i: (0, 0))
    row = pl.BlockSpec((t, SSD_WIDTH), lambda i: (i, 0))
    return pl.pallas_call(
        _ssd_out_kernel,
        grid=(n // t,),
        in_specs=[
            row, row, row,
            pl.BlockSpec((t, SSD_WIDTH), lambda i: (i, COL_Z // SSD_WIDTH)),
            vec, vec,
        ],
        out_specs=pl.BlockSpec((t, SSD_WIDTH), lambda i: (i, 0)),
        out_shape=jax.ShapeDtypeStruct((n, SSD_WIDTH), BF16),
        compiler_params=_cparams(("parallel",)),
        name="ssd_out",
    )(y_f, y_b, xa, p, jnp.repeat(d_skip, SSD_HEAD_DIM).reshape(1, SSD_WIDTH), norm_g.reshape(1, SSD_WIDTH))


def _out_proj_kernel(x_ref, a_ref, s_ref, c_ref, mod_ref, w_ref, o_ref, *, n_lat, tm):
    i = pl.program_id(0)
    acc = jnp.dot(a_ref[...], w_ref[0:ATTN_WIDTH, :], preferred_element_type=F32)
    acc += jnp.dot(s_ref[...], w_ref[ATTN_WIDTH:ATTN_WIDTH + SSD_WIDTH, :], preferred_element_type=F32)
    acc += jnp.dot(c_ref[...], w_ref[ATTN_WIDTH + SSD_WIDTH:, :], preferred_element_type=F32)
    for r in range(0, tm, ROW_CHUNK):
        rs = slice(r, r + ROW_CHUNK)
        o_ref[rs, :] = x_ref[rs, :] + _mod_row(mod_ref, 2, i * tm + r, n_lat) * acc[rs, :]


def _out_proj(xs, attn, ssd, cm, mod3, w_out, *, rows, tm, n_lat):
    d = xs.shape[1]
    return pl.pallas_call(
        functools.partial(_out_proj_kernel, n_lat=n_lat, tm=tm),
        grid=(rows // tm,),
        in_specs=[
            pl.BlockSpec((tm, d), lambda i: (i, 0)),
            pl.BlockSpec((tm, ATTN_WIDTH), lambda i: (i, 0)),
            pl.BlockSpec((tm, SSD_WIDTH), lambda i: (i, 0)),
            pl.BlockSpec((tm, CM_WIDTH), lambda i: (i, 0)),
            pl.BlockSpec((3, SUBLANES, d), lambda i: (0, 0, 0)),
            pl.BlockSpec((D_MIX, d), lambda i: (0, 0)),
        ],
        out_specs=pl.BlockSpec((tm, d), lambda i: (i, 0)),
        out_shape=jax.ShapeDtypeStruct((rows, d), F32),
        compiler_params=_cparams(("parallel",)),
        name="out_proj",
    )(xs, attn, ssd, cm, mod3, w_out)


def _rope_tables(n_lat, n):
    rows = n_lat // GRID_W
    row = jnp.repeat(jnp.arange(rows), GRID_W).astype(F32)
    col = jnp.tile(jnp.arange(GRID_W), rows).astype(F32)
    inv = ROPE_THETA ** (-jnp.arange(0, ROPE_AXIS_DIM, 2, dtype=F32) / ROPE_AXIS_DIM)
    ar = row[:, None] * inv
    ac = col[:, None] * inv
    cos = jnp.concatenate([jnp.cos(ar), jnp.cos(ac), jnp.cos(ar), jnp.cos(ac)], axis=1)
    sin = jnp.concatenate([-jnp.sin(ar), -jnp.sin(ac), jnp.sin(ar), jnp.sin(ac)], axis=1)
    pad = n - n_lat
    cos = jnp.concatenate([cos, jnp.ones((pad, HEAD_DIM), F32)], axis=0)
    sin = jnp.concatenate([sin, jnp.zeros((pad, HEAD_DIM), F32)], axis=0)
    return cos, sin


def _permute_head_dims(a):
    lead = a.shape[:-1]
    quarter = ROPE_AXIS_DIM // 2
    a = a.reshape(*lead, -1, 2, 2, quarter)
    return jnp.swapaxes(a, -3, -2).reshape(*lead, -1)


def _relayout_w_in(w_in):
    d = w_in.shape[0]
    o_k = ATTN_WIDTH
    o_v = o_k + KV_WIDTH
    o_z = o_v + KV_WIDTH
    o_xbc = o_z + SSD_WIDTH
    o_dt = o_xbc + SSD_CONV_CH
    o_glu = o_dt + 2 * SSD_HEADS
    dt = w_in[:, o_dt:o_glu]
    zpad = jnp.zeros((d, LANES - SSD_HEADS), w_in.dtype)
    tail = jnp.zeros((d, P_COLS - COL_DT - 2 * LANES), w_in.dtype)
    return jnp.concatenate([
        _permute_head_dims(w_in[:, :o_k]), _permute_head_dims(w_in[:, o_k:o_v]), w_in[:, o_v:o_z],
        w_in[:, o_xbc:o_dt], w_in[:, o_z:o_xbc], w_in[:, o_glu:],
        dt[:, :SSD_HEADS], zpad, dt[:, SSD_HEADS:], zpad, tail], axis=1).astype(BF16)


def kernel(x, c, ctx, c_ctx, w_mod, b_mod, norm_g, w_ffn_in, w_ffn_out, w_in, w_out, qk_g,
           ssd_conv_w, ssd_conv_b, ssd_dt_bias, ssd_a_log, ssd_d, ssd_norm_g,
           cm_dw_w, cm_dw_b, cm_ln_g, cm_ln_b, final_g):
    assert x.shape[0] == 1 and ctx.shape[0] == 1
    n_lat, d = x.shape[1], x.shape[2]
    n_ctx = ctx.shape[1]
    n = n_lat + n_ctx
    depth = w_mod.shape[0]
    assert n % MM_TILE == 0 and n_lat % ROW_TILE == 0 and n_ctx % ROW_TILE == 0
    assert ROW_TILE % ROW_CHUNK == 0 and MM_TILE % (2 * ROW_CHUNK) == 0
    kv_tile = max(t for t in range(KEY_CHUNK, KV_TILE + 1, KEY_CHUNK) if n % t == 0)
    lat_tm = 512
    assert n_lat % lat_tm == 0 and n_lat % GRID_W == 0

    xs = jnp.concatenate([x[0], ctx[0]], axis=0)
    cvec = jnp.concatenate([c_ctx[None, :], c, jnp.zeros((SUBLANES - 2, d), F32)], axis=0)
    mods = _modulation(cvec, w_mod, b_mod)
    mods = mods.reshape(depth, SUBLANES, N_MOD, d).transpose(0, 2, 1, 3)
    cos, sin = _rope_tables(n_lat, n)
    w_ffn_in_b = w_ffn_in.astype(BF16)
    w_ffn_out_b = w_ffn_out.astype(BF16)
    w_out_b = w_out.astype(BF16)

    out = None
    for l in range(depth):
        last = l == depth - 1
        m = mods[l]
        xs = _ffn(xs, m[0:3], norm_g[l, 0], w_ffn_in_b, w_ffn_out_b, l, 0,
                  rows=n, tm=MM_TILE, n_lat=n_lat)
        q, k, v, p = _in_proj(xs, m[3:6], norm_g[l, 1], _relayout_w_in(w_in[l]), cos, sin,
                              _permute_head_dims(qk_g[l]), tm=MM_TILE, n_lat=n_lat)
        attn = _attention(q, k, v, q_tiles=n_lat // Q_TILE, q_off=0, kv_tiles=n // kv_tile, kv_off=0,
                          tq=Q_TILE, tk=kv_tile)
        if not last:
            attn = _attention(q, k, v, q_tiles=n_ctx // ROW_TILE, q_off=n_lat // ROW_TILE,
                              kv_tiles=n_ctx // ROW_TILE, kv_off=n_lat // ROW_TILE,
                              tq=ROW_TILE, tk=ROW_TILE, prev=attn)
        xa = _ssd_conv(p, ssd_conv_w[l], ssd_conv_b[l], n_lat=n_lat)
        y_f, y_b = _ssd_scan(xa, p, ssd_dt_bias[l], ssd_a_log[l], n_lat=n_lat)
        ssd = _ssd_out(y_f, y_b, xa, p, ssd_d[l], ssd_norm_g[l])
        cm = _conv_module(p, cm_dw_w[l], cm_dw_b[l], cm_ln_g[l], cm_ln_b[l], n_lat=n_lat)
        if last:
            xs = _out_proj(xs, attn, ssd, cm, m[3:6], w_out_b[l], rows=n_lat, tm=lat_tm // 2, n_lat=n_lat)
            out = _ffn(xs, m[6:9], norm_g[l, 2], w_ffn_in_b, w_ffn_out_b, l, 1,
                       rows=n_lat, tm=lat_tm, n_lat=n_lat, final_g=final_g)
        else:
            xs = _out_proj(xs, attn, ssd, cm, m[3:6], w_out_b[l], rows=n, tm=MM_TILE // 2, n_lat=n_lat)
            xs = _ffn(xs, m[6:9], norm_g[l, 2], w_ffn_in_b, w_ffn_out_b, l, 1,
                      rows=n, tm=MM_TILE, n_lat=n_lat)
    return out[None]
```

```python
import functools
import math

import jax
import jax.numpy as jnp
from jax import lax
from jax.experimental import pallas as pl
from jax.experimental.pallas import tpu as pltpu

F32 = jnp.float32
BF16 = jnp.bfloat16

D_MODEL = 2048
DEPTH = 2
GRID_W = 64
N_MOD = 9
D_FF = 5632
EPS = 1e-6

HEAD_DIM = 128
ATTN_HEADS = 8
ATTN_KV_HEADS = 2
ATTN_GROUPS = ATTN_HEADS // ATTN_KV_HEADS
ATTN_WIDTH = ATTN_HEADS * HEAD_DIM
KV_WIDTH = ATTN_KV_HEADS * HEAD_DIM
ROPE_THETA = 10000.0
ROPE_AXIS_DIM = HEAD_DIM // 2

SSD_HEADS = 8
SSD_HEAD_DIM = 64
SSD_WIDTH = SSD_HEADS * SSD_HEAD_DIM
SSD_GROUPS = 2
SSD_STATE = 128
SSD_CONV = 7
SSD_CHUNK = 128
SSD_CONV_CH = SSD_WIDTH + 2 * SSD_GROUPS * SSD_STATE

CM_WIDTH = 512
CM_KERNEL = 31
D_MIX = ATTN_WIDTH + SSD_WIDTH + CM_WIDTH

LANES = 128
SUBLANES = 8
VMEM_LIMIT = 56 * 1024 * 1024

QKV_COLS = ATTN_WIDTH + 2 * KV_WIDTH
COL_XBC = 0
COL_Z = COL_XBC + SSD_CONV_CH
COL_GLU = COL_Z + SSD_WIDTH
COL_DT = COL_GLU + 2 * CM_WIDTH
P_COLS = 2 * QKV_COLS

ROW_TILE = 256
MM_TILE = 640
FF_CHUNK = 512
KV_TILE = 3328
Q_TILE = 256
EXP_LANES = 1024
SSD_HALO = 8
CM_HALO = 16
BF16_ROWS = 16
VT_ROWS = HEAD_DIM + BF16_ROWS
QK_EXP2_SCALE = (HEAD_DIM ** -0.5) * math.log2(math.e)
KEY_CHUNK = 256
SCORE_BUFFERS = 3
SCORE_AHEAD = 1


def _cparams(sem):
    return pltpu.CompilerParams(dimension_semantics=sem, vmem_limit_bytes=VMEM_LIMIT)


def _sigmoid(x):
    return 1.0 / (1.0 + jnp.exp(-x))


def _silu(x):
    return x * _sigmoid(x)


def _rms(x, g):
    return x * lax.rsqrt(jnp.mean(x * x, axis=-1, keepdims=True) + EPS) * g


ROW_CHUNK = 64


def _mod_row(mod_ref, k, row0, n_lat):
    cls = (row0 < n_lat).astype(jnp.int32)
    return mod_ref[k, pl.ds(cls, 1), :]


def _norm_mod_store(h_scr, x_ref, mod_ref, g_ref, row0, tm, n_lat, start=0):
    for r in range(start, tm, ROW_CHUNK):
        shift = _mod_row(mod_ref, 0, row0 + r, n_lat)
        gain = g_ref[...] * (1.0 + _mod_row(mod_ref, 1, row0 + r, n_lat))
        x = x_ref[r:r + ROW_CHUNK, :]
        inv = lax.rsqrt(jnp.mean(x * x, axis=-1, keepdims=True) + EPS)
        h_scr[r:r + ROW_CHUNK, :] = ((x * inv) * gain + shift).astype(BF16)


def _mod_kernel(c_ref, w_ref, b_ref, o_ref):
    a = _silu(c_ref[...]).astype(BF16)
    o_ref[0] = jnp.dot(a, w_ref[0].astype(BF16), preferred_element_type=F32) + b_ref[0]


def _modulation(cvec, w_mod, b_mod):
    depth, d, n = w_mod.shape
    tn = 1024
    return pl.pallas_call(
        _mod_kernel,
        grid=(depth, n // tn),
        in_specs=[
            pl.BlockSpec((SUBLANES, d), lambda l, j: (0, 0)),
            pl.BlockSpec((1, d, tn), lambda l, j: (l, 0, j)),
            pl.BlockSpec((1, 1, tn), lambda l, j: (l, 0, j)),
        ],
        out_specs=pl.BlockSpec((1, SUBLANES, tn), lambda l, j: (l, 0, j)),
        out_shape=jax.ShapeDtypeStruct((depth, SUBLANES, n), F32),
        compiler_params=_cparams(("arbitrary", "arbitrary")),
        name="modulation",
    )(cvec, w_mod, b_mod.reshape(depth, 1, n))


def _ffn_kernel(x_ref, mod_ref, g_ref, wg_ref, wu_ref, wo_ref, *rest, n_lat, tm, final):
    if final:
        fg_ref, o_ref, h_scr, acc_scr = rest
    else:
        o_ref, h_scr, acc_scr = rest
    i = pl.program_id(0)
    f = pl.program_id(1)
    last = pl.num_programs(1) - 1
    half = tm // 2

    def swiglu_chunk(rows, first):
        h = h_scr[rows, :]
        gt = jnp.dot(h, wg_ref[...], preferred_element_type=F32)
        up = jnp.dot(h, wu_ref[...], preferred_element_type=F32)
        out = jnp.dot((_silu(gt) * up).astype(BF16), wo_ref[...], preferred_element_type=F32)
        if first:
            acc_scr[rows, :] = out
        else:
            acc_scr[rows, :] += out

    def finish(start, stop):
        for r in range(start, stop, ROW_CHUNK):
            rs = slice(r, r + ROW_CHUNK)
            gate = _mod_row(mod_ref, 2, i * tm + r, n_lat)
            y = x_ref[rs, :] + (0.5 * gate) * acc_scr[rs, :]
            if final:
                y = _rms(y, fg_ref[...])
            o_ref[rs, :] = y

    @pl.when(f == 0)
    def _():
        _norm_mod_store(h_scr, x_ref, mod_ref, g_ref, i * tm, half, n_lat)
        swiglu_chunk(slice(0, half), True)
        _norm_mod_store(h_scr, x_ref, mod_ref, g_ref, i * tm, tm, n_lat, start=half)
        swiglu_chunk(slice(half, tm), True)

    @pl.when(jnp.logical_and(f > 0, f < last))
    def _():
        swiglu_chunk(slice(0, tm), False)

    @pl.when(f == last)
    def _():
        swiglu_chunk(slice(0, half), False)
        finish(0, half)
        swiglu_chunk(slice(half, tm), False)
        finish(half, tm)


def _ffn(xs, mod3, norm_g, w_in, w_out, layer, which, *, rows, tm, n_lat, final_g=None):
    d = xs.shape[1]
    nf = D_FF // FF_CHUNK
    final = final_g is not None
    in_specs = [
        pl.BlockSpec((tm, d), lambda i, f: (i, 0)),
        pl.BlockSpec((3, SUBLANES, d), lambda i, f: (0, 0, 0)),
        pl.BlockSpec((1, d), lambda i, f: (0, 0)),
        pl.BlockSpec((None, None, d, FF_CHUNK), lambda i, f: (layer, which, 0, f)),
        pl.BlockSpec((None, None, d, FF_CHUNK), lambda i, f: (layer, which, 0, nf + f)),
        pl.BlockSpec((None, None, FF_CHUNK, d), lambda i, f: (layer, which, f, 0)),
    ]
    args = [xs, mod3, norm_g.reshape(1, d), w_in, w_in, w_out]
    if final:
        in_specs.append(pl.BlockSpec((1, d), lambda i, f: (0, 0)))
        args.append(final_g.reshape(1, d))
    return pl.pallas_call(
        functools.partial(_ffn_kernel, n_lat=n_lat, tm=tm, final=final),
        grid=(rows // tm, nf),
        in_specs=in_specs,
        out_specs=pl.BlockSpec((tm, d), lambda i, f: (i, 0)),
        out_shape=jax.ShapeDtypeStruct((rows, d), F32),
        scratch_shapes=[pltpu.VMEM((tm, d), BF16), pltpu.VMEM((tm, d), F32)],
        compiler_params=_cparams(("parallel", "arbitrary")),
        name="ffn",
    )(*args)


def _proj_kernel(x_ref, mod_ref, g_ref, w_ref, cos_ref, sin_ref, qkg_ref, qo_ref, ko_ref, vo_ref, po_ref,
                 h_scr, qkv_scr, *, n_lat, tm):
    i = pl.program_id(0)
    j = pl.program_id(1)

    @pl.when(j == 0)
    def _():
        _norm_mod_store(h_scr, x_ref, mod_ref, g_ref, i * tm, tm, n_lat)
        qkv_scr[...] = jnp.dot(h_scr[...], w_ref[...], preferred_element_type=F32)

    @pl.when(j == 1)
    def _():
        po_ref[...] = jnp.dot(h_scr[...], w_ref[...], preferred_element_type=F32)
        cos = cos_ref[...]
        sin = sin_ref[...]

        def norm_rope(xh, g):
            xh = _rms(xh, g)
            return xh * cos + pltpu.roll(xh, HEAD_DIM // 2, 1) * sin

        for h in range(ATTN_HEADS):
            sl = slice(h * HEAD_DIM, (h + 1) * HEAD_DIM)
            qo_ref[:, sl] = (norm_rope(qkv_scr[:, sl], qkg_ref[0:1, :]) * QK_EXP2_SCALE).astype(BF16)
        for h in range(ATTN_KV_HEADS):
            sl = slice(ATTN_WIDTH + h * HEAD_DIM, ATTN_WIDTH + (h + 1) * HEAD_DIM)
            ko_ref[:, h * HEAD_DIM:(h + 1) * HEAD_DIM] = norm_rope(qkv_scr[:, sl], qkg_ref[1:2, :]).astype(BF16)
        vt = qkv_scr[:, ATTN_WIDTH + KV_WIDTH:].T.astype(BF16)
        ones = jnp.ones((VT_ROWS - HEAD_DIM, tm), BF16)
        for g in range(ATTN_KV_HEADS):
            vo_ref[g * VT_ROWS:g * VT_ROWS + HEAD_DIM, :] = vt[g * HEAD_DIM:(g + 1) * HEAD_DIM, :]
            vo_ref[g * VT_ROWS + HEAD_DIM:(g + 1) * VT_ROWS, :] = ones

    @pl.when(j > 1)
    def _():
        po_ref[...] = jnp.dot(h_scr[...], w_ref[...], preferred_element_type=F32)


def _in_proj(xs, mod3, norm_g, w_in_r, cos, sin, qk_g, *, tm, n_lat):
    n, d = xs.shape
    tn = QKV_COLS
    steps = (QKV_COLS + P_COLS) // tn
    return pl.pallas_call(
        functools.partial(_proj_kernel, n_lat=n_lat, tm=tm),
        grid=(n // tm, steps),
        in_specs=[
            pl.BlockSpec((tm, d), lambda i, j: (i, 0)),
            pl.BlockSpec((3, SUBLANES, d), lambda i, j: (0, 0, 0)),
            pl.BlockSpec((1, d), lambda i, j: (0, 0)),
            pl.BlockSpec((d, tn), lambda i, j: (0, j)),
            pl.BlockSpec((tm, HEAD_DIM), lambda i, j: (i, 0)),
            pl.BlockSpec((tm, HEAD_DIM), lambda i, j: (i, 0)),
            pl.BlockSpec((2, HEAD_DIM), lambda i, j: (0, 0)),
        ],
        out_specs=[
            pl.BlockSpec((tm, ATTN_WIDTH), lambda i, j: (i, 0)),
            pl.BlockSpec((tm, KV_WIDTH), lambda i, j: (i, 0)),
            pl.BlockSpec((ATTN_KV_HEADS * VT_ROWS, tm), lambda i, j: (0, i)),
            pl.BlockSpec((tm, tn), lambda i, j: (i, jnp.maximum(j - 1, 0))),
        ],
        out_shape=[
            jax.ShapeDtypeStruct((n, ATTN_WIDTH), BF16),
            jax.ShapeDtypeStruct((n, KV_WIDTH), BF16),
            jax.ShapeDtypeStruct((ATTN_KV_HEADS * VT_ROWS, n), BF16),
            jax.ShapeDtypeStruct((n, P_COLS), F32),
        ],
        scratch_shapes=[pltpu.VMEM((tm, d), BF16), pltpu.VMEM((tm, tn), F32)],
        compiler_params=_cparams(("parallel", "arbitrary")),
        name="in_proj",
    )(xs, mod3, norm_g.reshape(1, d), w_in_r, cos, sin, qk_g)


def _flash_kernel(*refs, aliased):
    if aliased:
        q_ref, k_ref, vt_ref, _, o_ref, m_scr, acc_scr, *bufs = refs
    else:
        q_ref, k_ref, vt_ref, o_ref, m_scr, acc_scr, *bufs = refs
    s_bufs, p_bufs = bufs[:len(bufs) // 2], bufs[len(bufs) // 2:]
    j = pl.program_id(1)
    tq = q_ref.shape[0]
    tk = k_ref.shape[0]
    w = ATTN_GROUPS * tq
    kc = min(KEY_CHUNK, tk)

    @pl.when(j == 0)
    def _():
        m_scr[...] = jnp.full_like(m_scr, -jnp.inf)
        acc_scr[...] = jnp.zeros_like(acc_scr)

    qs = [jnp.concatenate(
        [q_ref[:, (ATTN_GROUPS * g + h) * HEAD_DIM:(ATTN_GROUPS * g + h + 1) * HEAD_DIM]
         for h in range(ATTN_GROUPS)], axis=0) for g in range(ATTN_KV_HEADS)]
    units = [(g, ci) for ci in range(tk // kc) for g in range(ATTN_KV_HEADS)]

    def scores(u):
        g, ci = units[u]
        st = lax.dot_general(k_ref[ci * kc:(ci + 1) * kc, g * HEAD_DIM:(g + 1) * HEAD_DIM], qs[g],
                             (((1,), (1,)), ((), ())), preferred_element_type=F32)
        s_bufs[u % len(s_bufs)][...] = st
        mx = st[0:SUBLANES]
        for r in range(1, kc // SUBLANES):
            mx = jnp.maximum(mx, st[r * SUBLANES:(r + 1) * SUBLANES])
        return mx

    m_run = [m_scr[g] for g in range(ATTN_KV_HEADS)]
    row0 = jnp.minimum(j, 0)
    pending = [scores(u) for u in range(min(SCORE_AHEAD, len(units)))]
    for u, (g, ci) in enumerate(units):
        mx = pending.pop(0)
        if u + SCORE_AHEAD < len(units):
            pending.append(scores(u + SCORE_AHEAD))
        m_new = jnp.maximum(m_run[g], jnp.max(mx, axis=0, keepdims=True))
        alpha = jnp.exp2(m_run[g] - m_new)
        s_buf, p_buf = s_bufs[u % len(s_bufs)], p_bufs[u % len(p_bufs)]
        for c0 in range(0, w, EXP_LANES):
            cs = slice(c0, min(c0 + EXP_LANES, w))
            m_rep = jnp.broadcast_to(m_new[:, cs], (BF16_ROWS, cs.stop - cs.start))
            for r in range(kc // BF16_ROWS):
                rs = slice(r * BF16_ROWS, (r + 1) * BF16_ROWS)
                rd = pl.ds(pl.multiple_of(row0 + r * BF16_ROWS, BF16_ROWS), BF16_ROWS)
                p_buf[rs, cs] = jnp.exp2(s_buf[rd, cs] - m_rep).astype(BF16)
        pv = jnp.dot(vt_ref[g * VT_ROWS:(g + 1) * VT_ROWS, ci * kc:(ci + 1) * kc], p_buf[...],
                     preferred_element_type=F32)
        acc_scr[g] = alpha * acc_scr[g] + pv
        m_run[g] = m_new
    for g in range(ATTN_KV_HEADS):
        m_scr[g] = m_run[g]

    @pl.when(j == pl.num_programs(1) - 1)
    def _():
        for g in range(ATTN_KV_HEADS):
            acc = acc_scr[g]
            ot = acc[0:HEAD_DIM] / acc[HEAD_DIM:HEAD_DIM + 1]
            for h in range(ATTN_GROUPS):
                col = (ATTN_GROUPS * g + h) * HEAD_DIM
                o_ref[:, col:col + HEAD_DIM] = ot[:, h * tq:(h + 1) * tq].T.astype(o_ref.dtype)


def _attention(q, k, vt, *, q_tiles, q_off, kv_tiles, kv_off, tq, tk, prev=None):
    n = q.shape[0]
    aliased = prev is not None
    in_specs = [
        pl.BlockSpec((tq, ATTN_WIDTH), lambda i, j: (q_off + i, 0)),
        pl.BlockSpec((tk, KV_WIDTH), lambda i, j: (kv_off + j, 0)),
        pl.BlockSpec((ATTN_KV_HEADS * VT_ROWS, tk), lambda i, j: (0, kv_off + j)),
    ]
    args = [q, k, vt]
    if aliased:
        in_specs.append(pl.BlockSpec(memory_space=pl.ANY))
        args.append(prev)
    w = ATTN_GROUPS * tq
    kc = min(KEY_CHUNK, tk)
    units = min(ATTN_KV_HEADS * (tk // kc), SCORE_BUFFERS)
    return pl.pallas_call(
        functools.partial(_flash_kernel, aliased=aliased),
        grid=(q_tiles, kv_tiles),
        in_specs=in_specs,
        out_specs=pl.BlockSpec((tq, ATTN_WIDTH), lambda i, j: (q_off + i, 0)),
        out_shape=jax.ShapeDtypeStruct((n, ATTN_WIDTH), BF16),
        scratch_shapes=[
            pltpu.VMEM((ATTN_KV_HEADS, 1, w), F32),
            pltpu.VMEM((ATTN_KV_HEADS, VT_ROWS, w), F32),
        ] + [pltpu.VMEM((kc, w), F32)] * units + [pltpu.VMEM((kc, w), BF16)] * units,
        input_output_aliases={3: 0} if aliased else {},
        compiler_params=_cparams(("parallel", "arbitrary")),
        name="attention_ctx" if aliased else "attention",
    )(*args)


def _segment_edges(i, lat_tiles):
    first = jnp.logical_or(i == 0, i == lat_tiles)
    last = jnp.logical_or(i == lat_tiles - 1, i == pl.num_programs(0) - 1)
    return first, last


def _ssd_conv_kernel(prev_ref, cur_ref, next_ref, w_ref, b_ref, o_ref, ext_scr, *, lat_tiles):
    i = pl.program_id(0)
    t = cur_ref.shape[0]
    first, last = _segment_edges(i, lat_tiles)
    ext_scr[0:SSD_HALO, :] = jnp.where(first, 0.0, prev_ref[...])
    ext_scr[SSD_HALO:SSD_HALO + t, :] = cur_ref[...]
    ext_scr[SSD_HALO + t:, :] = jnp.where(last, 0.0, next_ref[...])
    pad = SSD_CONV // 2
    acc = jnp.broadcast_to(b_ref[...], cur_ref.shape)
    for k in range(SSD_CONV):
        acc = acc + w_ref[k:k + 1, :] * ext_scr[pl.ds(SSD_HALO - pad + k, t), :]
    o_ref[...] = _silu(acc)


def _ssd_conv(p, conv_w, conv_b, *, n_lat):
    n = p.shape[0]
    t = ROW_TILE
    hb = t // SSD_HALO
    last_hb = n // SSD_HALO - 1
    cb = COL_XBC // SSD_CONV_CH
    return pl.pallas_call(
        functools.partial(_ssd_conv_kernel, lat_tiles=n_lat // t),
        grid=(n // t,),
        in_specs=[
            pl.BlockSpec((SSD_HALO, SSD_CONV_CH), lambda i: (jnp.maximum(i * hb - 1, 0), cb)),
            pl.BlockSpec((t, SSD_CONV_CH), lambda i: (i, cb)),
            pl.BlockSpec((SSD_HALO, SSD_CONV_CH), lambda i: (jnp.minimum((i + 1) * hb, last_hb), cb)),
            pl.BlockSpec((SSD_CONV, SSD_CONV_CH), lambda i: (0, 0)),
            pl.BlockSpec((1, SSD_CONV_CH), lambda i: (0, 0)),
        ],
        out_specs=pl.BlockSpec((t, SSD_CONV_CH), lambda i: (i, 0)),
        out_shape=jax.ShapeDtypeStruct((n, SSD_CONV_CH), F32),
        scratch_shapes=[pltpu.VMEM((t + 2 * SSD_HALO, SSD_CONV_CH), F32)],
        compiler_params=_cparams(("parallel",)),
        name="ssd_conv",
    )(p, p, p, conv_w, conv_b.reshape(1, SSD_CONV_CH))


def _cm_kernel(ap_ref, gp_ref, a_ref, g_ref, an_ref, gn_ref, w_ref, b_ref, lg_ref, lb_ref, o_ref, ext_scr,
               sh_scr, *, lat_tiles):
    i = pl.program_id(0)
    t = a_ref.shape[0]
    first, last = _segment_edges(i, lat_tiles)

    def glu(a, gt):
        return a[...] * _sigmoid(gt[...])

    ext_scr[0:CM_HALO, :] = jnp.where(first, 0.0, glu(ap_ref, gp_ref))
    ext_scr[CM_HALO:CM_HALO + t, :] = glu(a_ref, g_ref)
    ext_scr[CM_HALO + t:, :] = jnp.where(last, 0.0, glu(an_ref, gn_ref))
    pad = CM_KERNEL // 2
    first_off = CM_HALO - pad
    span = t + (first_off + CM_KERNEL - 1) // SUBLANES * SUBLANES
    acc = jnp.broadcast_to(b_ref[...], a_ref.shape)
    for phase in range(SUBLANES):
        taps = [k for k in range(CM_KERNEL) if (first_off + k) % SUBLANES == phase]
        if not taps:
            continue
        shifted = sh_scr.at[phase % 2]
        shifted[...] = ext_scr[phase:phase + span, :]
        for k in taps:
            row = first_off + k - phase
            acc = acc + w_ref[k:k + 1, :] * shifted[row:row + t, :]
    mu = jnp.mean(acc, axis=-1, keepdims=True)
    cen = acc - mu
    var = jnp.mean(cen * cen, axis=-1, keepdims=True)
    y = cen * lax.rsqrt(var + EPS) * lg_ref[...] + lb_ref[...]
    o_ref[...] = _silu(y).astype(o_ref.dtype)


def _conv_module(p, dw_w, dw_b, ln_g, ln_b, *, n_lat):
    n = p.shape[0]
    t = ROW_TILE
    hb = t // CM_HALO
    last_hb = n // CM_HALO - 1
    ca = COL_GLU // CM_WIDTH
    cg = ca + 1

    def prev_map(c):
        return lambda i: (jnp.maximum(i * hb - 1, 0), c)

    def next_map(c):
        return lambda i: (jnp.minimum((i + 1) * hb, last_hb), c)

    vec = pl.BlockSpec((1, CM_WIDTH), lambda i: (0, 0))
    return pl.pallas_call(
        functools.partial(_cm_kernel, lat_tiles=n_lat // t),
        grid=(n // t,),
        in_specs=[
            pl.BlockSpec((CM_HALO, CM_WIDTH), prev_map(ca)),
            pl.BlockSpec((CM_HALO, CM_WIDTH), prev_map(cg)),
            pl.BlockSpec((t, CM_WIDTH), lambda i: (i, ca)),
            pl.BlockSpec((t, CM_WIDTH), lambda i: (i, cg)),
            pl.BlockSpec((CM_HALO, CM_WIDTH), next_map(ca)),
            pl.BlockSpec((CM_HALO, CM_WIDTH), next_map(cg)),
            pl.BlockSpec((CM_KERNEL, CM_WIDTH), lambda i: (0, 0)),
            vec, vec, vec,
        ],
        out_specs=pl.BlockSpec((t, CM_WIDTH), lambda i: (i, 0)),
        out_shape=jax.ShapeDtypeStruct((n, CM_WIDTH), BF16),
        scratch_shapes=[pltpu.VMEM((t + 2 * CM_HALO, CM_WIDTH), F32),
                        pltpu.VMEM((2, t + 2 * CM_HALO - SUBLANES, CM_WIDTH), F32)],
        compiler_params=_cparams(("parallel",)),
        name="conv_module",
    )(p, p, p, p, p, p, dw_w, dw_b.reshape(1, -1), ln_g.reshape(1, -1), ln_b.reshape(1, -1))


def _split3_dot(tri, x):
    hi = x.astype(BF16)
    r1 = x - hi.astype(F32)
    mid = r1.astype(BF16)
    lo = (r1 - mid.astype(F32)).astype(BF16)
    return (jnp.dot(tri, hi, preferred_element_type=F32) + jnp.dot(tri, mid, preferred_element_type=F32)
            + jnp.dot(tri, lo, preferred_element_type=F32))


def _ssd_chunk(xa_ref, dt_ref, bias, alog, y_ref, h_scr, backward):
    q = SSD_CHUNK
    x = dt_ref[...] + bias
    dt = jnp.maximum(x, 0.0) + jnp.log1p(jnp.exp(-jnp.abs(x)))
    da = dt * (-jnp.exp(alog))
    ii = lax.broadcasted_iota(jnp.int32, (q, q), 0)
    jj = lax.broadcasted_iota(jnp.int32, (q, q), 1)
    lane = jj
    inc = (jj >= ii) if backward else (jj <= ii)
    tri = jnp.where(inc, 1.0, 0.0).astype(BF16)
    cum = _split3_dot(tri, da)
    tot = jnp.sum(da, axis=0, keepdims=True)
    cum_t = cum.T
    dt_t = dt.T
    e_in = jnp.exp(cum)
    w_out = jnp.exp(tot - cum) * dt
    e_tot = jnp.exp(tot)

    xa = xa_ref[...]
    pair = 2 * SSD_HEAD_DIM
    low = lane < SSD_HEAD_DIM
    for g in range(SSD_GROUPS):
        b = xa[:, SSD_WIDTH + g * SSD_STATE:SSD_WIDTH + (g + 1) * SSD_STATE].astype(BF16)
        cg = xa[:, SSD_WIDTH + (SSD_GROUPS + g) * SSD_STATE:
                SSD_WIDTH + (SSD_GROUPS + g + 1) * SSD_STATE].astype(BF16)
        cb = lax.dot_general(cg, b, (((1,), (1,)), ((), ())), preferred_element_type=F32)
        for m in range(SSD_HEADS // SSD_GROUPS // 2):
            pi = g * (SSD_HEADS // SSD_GROUPS // 2) + m
            h0, h1 = 2 * pi, 2 * pi + 1
            xp = xa[:, pi * pair:(pi + 1) * pair]
            xp_b = xp.astype(BF16)
            ys = []
            for hd in (h0, h1):
                seg = cum[:, hd:hd + 1] - cum_t[hd:hd + 1, :]
                dec = jnp.exp(jnp.where(inc, seg, -jnp.inf))
                sc = (cb * dec * dt_t[hd:hd + 1, :]).astype(BF16)
                ys.append(jnp.dot(sc, xp_b, preferred_element_type=F32))
            y_diag = jnp.where(low, ys[0], ys[1])
            hp = h_scr[pi]
            y_off = lax.dot_general(cg, hp.astype(BF16), (((1,), (1,)), ((), ())),
                                    preferred_element_type=F32)
            y_off = y_off * jnp.where(low, e_in[:, h0:h0 + 1], e_in[:, h1:h1 + 1])
            y_ref[:, pi * pair:(pi + 1) * pair] = y_diag + y_off
            xw = xp * jnp.where(low, w_out[:, h0:h0 + 1], w_out[:, h1:h1 + 1])
            st = jnp.dot(xw.T.astype(BF16), b, preferred_element_type=F32)
            row_low = ii < SSD_HEAD_DIM
            keep = jnp.where(row_low, e_tot[:, h0:h0 + 1], e_tot[:, h1:h1 + 1])
            h_scr[pi] = keep * hp + st


def _ssd_scan_kernel(xaf_ref, xab_ref, dtf_ref, dtb_ref, bias_ref, alog_ref, yf_ref, yb_ref, h_scr):
    @pl.when(pl.program_id(0) == 0)
    def _():
        h_scr[...] = jnp.zeros_like(h_scr)

    _ssd_chunk(xaf_ref, dtf_ref, bias_ref[0], alog_ref[0], yf_ref, h_scr.at[0], False)
    _ssd_chunk(xab_ref, dtb_ref, bias_ref[1], alog_ref[1], yb_ref, h_scr.at[1], True)


def _ssd_scan(xa, p, dt_bias, a_log, *, n_lat):
    n = xa.shape[0]
    nc = n // SSD_CHUNK
    lat_c = n_lat // SSD_CHUNK
    ctx_c = nc - lat_c
    dt_cb = COL_DT // LANES

    def fwd(s):
        return jnp.where(s < ctx_c, lat_c + s, s - ctx_c)

    def bwd(s):
        return nc - 1 - s

    pad = LANES - SSD_HEADS
    bias = jnp.pad(dt_bias, ((0, 0), (0, pad))).reshape(2, 1, LANES)
    alog = jnp.pad(a_log, ((0, 0), (0, pad))).reshape(2, 1, LANES)
    y_sds = jax.ShapeDtypeStruct((n, SSD_WIDTH), F32)
    return pl.pallas_call(
        _ssd_scan_kernel,
        grid=(nc,),
        in_specs=[
            pl.BlockSpec((SSD_CHUNK, SSD_CONV_CH), lambda s: (fwd(s), 0)),
            pl.BlockSpec((SSD_CHUNK, SSD_CONV_CH), lambda s: (bwd(s), 0)),
            pl.BlockSpec((SSD_CHUNK, LANES), lambda s: (fwd(s), dt_cb)),
            pl.BlockSpec((SSD_CHUNK, LANES), lambda s: (bwd(s), dt_cb + 1)),
            pl.BlockSpec((2, 1, LANES), lambda s: (0, 0, 0)),
            pl.BlockSpec((2, 1, LANES), lambda s: (0, 0, 0)),
        ],
        out_specs=[
            pl.BlockSpec((SSD_CHUNK, SSD_WIDTH), lambda s: (fwd(s), 0)),
            pl.BlockSpec((SSD_CHUNK, SSD_WIDTH), lambda s: (bwd(s), 0)),
        ],
        out_shape=[y_sds, y_sds],
        scratch_shapes=[pltpu.VMEM((2, SSD_HEADS // 2, 2 * SSD_HEAD_DIM, SSD_STATE), F32)],
        compiler_params=_cparams(("arbitrary",)),
        name="ssd_scan",
    )(xa, xa, p, p, bias, alog)


def _ssd_out_kernel(yf_ref, yb_ref, xs_ref, z_ref, d_ref, g_ref, o_ref):
    y = yf_ref[...] + yb_ref[...] + d_ref[...] * xs_ref[...]
    y = y * _silu(z_ref[...])
    o_ref[...] = _rms(y, g_ref[...]).astype(o_ref.dtype)


def _ssd_out(y_f, y_b, xa, p, d_skip, norm_g):
    n = xa.shape[0]
    t = ROW_TILE
    vec = pl.BlockSpec((1, SSD_WIDTH), lambda i: (0, 0))
    row = pl.BlockSpec((t, SSD_WIDTH), lambda i: (i, 0))
    return pl.pallas_call(
        _ssd_out_kernel,
        grid=(n // t,),
        in_specs=[
            row, row, row,
            pl.BlockSpec((t, SSD_WIDTH), lambda i: (i, COL_Z // SSD_WIDTH)),
            vec, vec,
        ],
        out_specs=pl.BlockSpec((t, SSD_WIDTH), lambda i: (i, 0)),
        out_shape=jax.ShapeDtypeStruct((n, SSD_WIDTH), BF16),
        compiler_params=_cparams(("parallel",)),
        name="ssd_out",
    )(y_f, y_b, xa, p, jnp.repeat(d_skip, SSD_HEAD_DIM).reshape(1, SSD_WIDTH), norm_g.reshape(1, SSD_WIDTH))


def _out_proj_kernel(x_ref, a_ref, s_ref, c_ref, mod_ref, w_ref, o_ref, *, n_lat, tm):
    i = pl.program_id(0)
    acc = jnp.dot(a_ref[...], w_ref[0:ATTN_WIDTH, :], preferred_element_type=F32)
    acc += jnp.dot(s_ref[...], w_ref[ATTN_WIDTH:ATTN_WIDTH + SSD_WIDTH, :], preferred_element_type=F32)
    acc += jnp.dot(c_ref[...], w_ref[ATTN_WIDTH + SSD_WIDTH:, :], preferred_element_type=F32)
    for r in range(0, tm, ROW_CHUNK):
        rs = slice(r, r + ROW_CHUNK)
        o_ref[rs, :] = x_ref[rs, :] + _mod_row(mod_ref, 2, i * tm + r, n_lat) * acc[rs, :]


def _out_proj(xs, attn, ssd, cm, mod3, w_out, *, rows, tm, n_lat):
    d = xs.shape[1]
    return pl.pallas_call(
        functools.partial(_out_proj_kernel, n_lat=n_lat, tm=tm),
        grid=(rows // tm,),
        in_specs=[
            pl.BlockSpec((tm, d), lambda i: (i, 0)),
            pl.BlockSpec((tm, ATTN_WIDTH), lambda i: (i, 0)),
            pl.BlockSpec((tm, SSD_WIDTH), lambda i: (i, 0)),
            pl.BlockSpec((tm, CM_WIDTH), lambda i: (i, 0)),
            pl.BlockSpec((3, SUBLANES, d), lambda i: (0, 0, 0)),
            pl.BlockSpec((D_MIX, d), lambda i: (0, 0)),
        ],
        out_specs=pl.BlockSpec((tm, d), lambda i: (i, 0)),
        out_shape=jax.ShapeDtypeStruct((rows, d), F32),
        compiler_params=_cparams(("parallel",)),
        name="out_proj",
    )(xs, attn, ssd, cm, mod3, w_out)


def _rope_tables(n_lat, n):
    rows = n_lat // GRID_W
    row = jnp.repeat(jnp.arange(rows), GRID_W).astype(F32)
    col = jnp.tile(jnp.arange(GRID_W), rows).astype(F32)
    inv = ROPE_THETA ** (-jnp.arange(0, ROPE_AXIS_DIM, 2, dtype=F32) / ROPE_AXIS_DIM)
    ar = row[:, None] * inv
    ac = col[:, None] * inv
    cos = jnp.concatenate([jnp.cos(ar), jnp.cos(ac), jnp.cos(ar), jnp.cos(ac)], axis=1)
    sin = jnp.concatenate([-jnp.sin(ar), -jnp.sin(ac), jnp.sin(ar), jnp.sin(ac)], axis=1)
    pad = n - n_lat
    cos = jnp.concatenate([cos, jnp.ones((pad, HEAD_DIM), F32)], axis=0)
    sin = jnp.concatenate([sin, jnp.zeros((pad, HEAD_DIM), F32)], axis=0)
    return cos, sin


def _permute_head_dims(a):
    lead = a.shape[:-1]
    quarter = ROPE_AXIS_DIM // 2
    a = a.reshape(*lead, -1, 2, 2, quarter)
    return jnp.swapaxes(a, -3, -2).reshape(*lead, -1)


def _relayout_w_in(w_in):
    d = w_in.shape[0]
    o_k = ATTN_WIDTH
    o_v = o_k + KV_WIDTH
    o_z = o_v + KV_WIDTH
    o_xbc = o_z + SSD_WIDTH
    o_dt = o_xbc + SSD_CONV_CH
    o_glu = o_dt + 2 * SSD_HEADS
    dt = w_in[:, o_dt:o_glu]
    zpad = jnp.zeros((d, LANES - SSD_HEADS), w_in.dtype)
    tail = jnp.zeros((d, P_COLS - COL_DT - 2 * LANES), w_in.dtype)
    return jnp.concatenate([
        _permute_head_dims(w_in[:, :o_k]), _permute_head_dims(w_in[:, o_k:o_v]), w_in[:, o_v:o_z],
        w_in[:, o_xbc:o_dt], w_in[:, o_z:o_xbc], w_in[:, o_glu:],
        dt[:, :SSD_HEADS], zpad, dt[:, SSD_HEADS:], zpad, tail], axis=1).astype(BF16)


def kernel(x, c, ctx, c_ctx, w_mod, b_mod, norm_g, w_ffn_in, w_ffn_out, w_in, w_out, qk_g,
           ssd_conv_w, ssd_conv_b, ssd_dt_bias, ssd_a_log, ssd_d, ssd_norm_g,
           cm_dw_w, cm_dw_b, cm_ln_g, cm_ln_b, final_g):
    assert x.shape[0] == 1 and ctx.shape[0] == 1
    n_lat, d = x.shape[1], x.shape[2]
    n_ctx = ctx.shape[1]
    n = n_lat + n_ctx
    depth = w_mod.shape[0]
    assert n % MM_TILE == 0 and n_lat % ROW_TILE == 0 and n_ctx % ROW_TILE == 0
    assert ROW_TILE % ROW_CHUNK == 0 and MM_TILE % (2 * ROW_CHUNK) == 0
    kv_tile = max(t for t in range(KEY_CHUNK, KV_TILE + 1, KEY_CHUNK) if n % t == 0)
    lat_tm = 512
    assert n_lat % lat_tm == 0 and n_lat % GRID_W == 0

    xs = jnp.concatenate([x[0], ctx[0]], axis=0)
    cvec = jnp.concatenate([c_ctx[None, :], c, jnp.zeros((SUBLANES - 2, d), F32)], axis=0)
    mods = _modulation(cvec, w_mod, b_mod)
    mods = mods.reshape(depth, SUBLANES, N_MOD, d).transpose(0, 2, 1, 3)
    cos, sin = _rope_tables(n_lat, n)
    w_ffn_in_b = w_ffn_in.astype(BF16)
    w_ffn_out_b = w_ffn_out.astype(BF16)
    w_out_b = w_out.astype(BF16)

    out = None
    for l in range(depth):
        last = l == depth - 1
        m = mods[l]
        xs = _ffn(xs, m[0:3], norm_g[l, 0], w_ffn_in_b, w_ffn_out_b, l, 0,
                  rows=n, tm=MM_TILE, n_lat=n_lat)
        q, k, v, p = _in_proj(xs, m[3:6], norm_g[l, 1], _relayout_w_in(w_in[l]), cos, sin,
                              _permute_head_dims(qk_g[l]), tm=MM_TILE, n_lat=n_lat)
        attn = _attention(q, k, v, q_tiles=n_lat // Q_TILE, q_off=0, kv_tiles=n // kv_tile, kv_off=0,
                          tq=Q_TILE, tk=kv_tile)
        if not last:
            attn = _attention(q, k, v, q_tiles=n_ctx // ROW_TILE, q_off=n_lat // ROW_TILE,
                              kv_tiles=n_ctx // ROW_TILE, kv_off=n_lat // ROW_TILE,
                              tq=ROW_TILE, tk=ROW_TILE, prev=attn)
        xa = _ssd_conv(p, ssd_conv_w[l], ssd_conv_b[l], n_lat=n_lat)
        y_f, y_b = _ssd_scan(xa, p, ssd_dt_bias[l], ssd_a_log[l], n_lat=n_lat)
        ssd = _ssd_out(y_f, y_b, xa, p, ssd_d[l], ssd_norm_g[l])
        cm = _conv_module(p, cm_dw_w[l], cm_dw_b[l], cm_ln_g[l], cm_ln_b[l], n_lat=n_lat)
        if last:
            xs = _out_proj(xs, attn, ssd, cm, m[3:6], w_out_b[l], rows=n_lat, tm=lat_tm // 2, n_lat=n_lat)
            out = _ffn(xs, m[6:9], norm_g[l, 2], w_ffn_in_b, w_ffn_out_b, l, 1,
                       rows=n_lat, tm=lat_tm, n_lat=n_lat, final_g=final_g)
        else:
            xs = _out_proj(xs, attn, ssd, cm, m[3:6], w_out_b[l], rows=n, tm=MM_TILE // 2, n_lat=n_lat)
            xs = _ffn(xs, m[6:9], norm_g[l, 2], w_ffn_in_b, w_ffn_out_b, l, 1,
                      rows=n, tm=MM_TILE, n_lat=n_lat)
    return out[None]
```

```python
import functools
import math

import jax
import jax.numpy as jnp
from jax import lax
from jax.experimental import pallas as pl
from jax.experimental.pallas import tpu as pltpu

F32 = jnp.float32
BF16 = jnp.bfloat16

D_MODEL = 2048
DEPTH = 2
GRID_W = 64
N_MOD = 9
D_FF = 5632
EPS = 1e-6

HEAD_DIM = 128
ATTN_HEADS = 8
ATTN_KV_HEADS = 2
ATTN_GROUPS = ATTN_HEADS // ATTN_KV_HEADS
ATTN_WIDTH = ATTN_HEADS * HEAD_DIM
KV_WIDTH = ATTN_KV_HEADS * HEAD_DIM
ROPE_THETA = 10000.0
ROPE_AXIS_DIM = HEAD_DIM // 2

SSD_HEADS = 8
SSD_HEAD_DIM = 64
SSD_WIDTH = SSD_HEADS * SSD_HEAD_DIM
SSD_GROUPS = 2
SSD_STATE = 128
SSD_CONV = 7
SSD_CHUNK = 128
SSD_CONV_CH = SSD_WIDTH + 2 * SSD_GROUPS * SSD_STATE

CM_WIDTH = 512
CM_KERNEL = 31
D_MIX = ATTN_WIDTH + SSD_WIDTH + CM_WIDTH

LANES = 128
SUBLANES = 8
VMEM_LIMIT = 56 * 1024 * 1024

QKV_COLS = ATTN_WIDTH + 2 * KV_WIDTH
COL_XBC = 0
COL_Z = COL_XBC + SSD_CONV_CH
COL_GLU = COL_Z + SSD_WIDTH
COL_DT = COL_GLU + 2 * CM_WIDTH
P_COLS = 2 * QKV_COLS

ROW_TILE = 256
MM_TILE = 640
FF_CHUNK = 512
KV_TILE = 3328
Q_TILE = 256
EXP_LANES = 1024
SSD_HALO = 8
CM_HALO = 16
CM_ROWS = 64
BF16_ROWS = 16
VT_ROWS = HEAD_DIM + BF16_ROWS
QK_EXP2_SCALE = (HEAD_DIM ** -0.5) * math.log2(math.e)
KEY_CHUNK = 256
SCORE_BUFFERS = 3
SCORE_AHEAD = 1


def _cparams(sem):
    return pltpu.CompilerParams(dimension_semantics=sem, vmem_limit_bytes=VMEM_LIMIT)


def _sigmoid(x):
    return 1.0 / (1.0 + jnp.exp(-x))


def _silu(x):
    return x * _sigmoid(x)


def _rms(x, g):
    return x * lax.rsqrt(jnp.mean(x * x, axis=-1, keepdims=True) + EPS) * g


ROW_CHUNK = 64


def _mod_row(mod_ref, k, row0, n_lat):
    cls = (row0 < n_lat).astype(jnp.int32)
    return mod_ref[k, pl.ds(cls, 1), :]


def _norm_mod_store(h_scr, x_ref, mod_ref, g_ref, row0, tm, n_lat, start=0):
    for r in range(start, tm, ROW_CHUNK):
        shift = _mod_row(mod_ref, 0, row0 + r, n_lat)
        gain = g_ref[...] * (1.0 + _mod_row(mod_ref, 1, row0 + r, n_lat))
        x = x_ref[r:r + ROW_CHUNK, :]
        inv = lax.rsqrt(jnp.mean(x * x, axis=-1, keepdims=True) + EPS)
        h_scr[r:r + ROW_CHUNK, :] = ((x * inv) * gain + shift).astype(BF16)


def _mod_kernel(c_ref, w_ref, b_ref, o_ref):
    a = _silu(c_ref[...]).astype(BF16)
    o_ref[0] = jnp.dot(a, w_ref[0].astype(BF16), preferred_element_type=F32) + b_ref[0]


def _modulation(cvec, w_mod, b_mod):
    depth, d, n = w_mod.shape
    tn = 1024
    return pl.pallas_call(
        _mod_kernel,
        grid=(depth, n // tn),
        in_specs=[
            pl.BlockSpec((SUBLANES, d), lambda l, j: (0, 0)),
            pl.BlockSpec((1, d, tn), lambda l, j: (l, 0, j)),
            pl.BlockSpec((1, 1, tn), lambda l, j: (l, 0, j)),
        ],
        out_specs=pl.BlockSpec((1, SUBLANES, tn), lambda l, j: (l, 0, j)),
        out_shape=jax.ShapeDtypeStruct((depth, SUBLANES, n), F32),
        compiler_params=_cparams(("arbitrary", "arbitrary")),
        name="modulation",
    )(cvec, w_mod, b_mod.reshape(depth, 1, n))


def _ffn_kernel(x_ref, mod_ref, g_ref, wg_ref, wu_ref, wo_ref, *rest, n_lat, tm, final):
    if final:
        fg_ref, o_ref, h_scr, acc_scr = rest
    else:
        o_ref, h_scr, acc_scr = rest
    i = pl.program_id(0)
    f = pl.program_id(1)
    last = pl.num_programs(1) - 1
    half = tm // 2

    def swiglu_chunk(rows, first):
        h = h_scr[rows, :]
        gt = jnp.dot(h, wg_ref[...], preferred_element_type=F32)
        up = jnp.dot(h, wu_ref[...], preferred_element_type=F32)
        out = jnp.dot((_silu(gt) * up).astype(BF16), wo_ref[...], preferred_element_type=F32)
        if first:
            acc_scr[rows, :] = out
        else:
            acc_scr[rows, :] += out

    def finish(start, stop):
        for r in range(start, stop, ROW_CHUNK):
            rs = slice(r, r + ROW_CHUNK)
            gate = _mod_row(mod_ref, 2, i * tm + r, n_lat)
            y = x_ref[rs, :] + (0.5 * gate) * acc_scr[rs, :]
            if final:
                y = _rms(y, fg_ref[...])
            o_ref[rs, :] = y

    @pl.when(f == 0)
    def _():
        _norm_mod_store(h_scr, x_ref, mod_ref, g_ref, i * tm, half, n_lat)
        swiglu_chunk(slice(0, half), True)
        _norm_mod_store(h_scr, x_ref, mod_ref, g_ref, i * tm, tm, n_lat, start=half)
        swiglu_chunk(slice(half, tm), True)

    @pl.when(jnp.logical_and(f > 0, f < last))
    def _():
        swiglu_chunk(slice(0, tm), False)

    @pl.when(f == last)
    def _():
        swiglu_chunk(slice(0, half), False)
        finish(0, half)
        swiglu_chunk(slice(half, tm), False)
        finish(half, tm)


def _ffn(xs, mod3, norm_g, w_in, w_out, layer, which, *, rows, tm, n_lat, final_g=None):
    d = xs.shape[1]
    nf = D_FF // FF_CHUNK
    final = final_g is not None
    in_specs = [
        pl.BlockSpec((tm, d), lambda i, f: (i, 0)),
        pl.BlockSpec((3, SUBLANES, d), lambda i, f: (0, 0, 0)),
        pl.BlockSpec((1, d), lambda i, f: (0, 0)),
        pl.BlockSpec((None, None, d, FF_CHUNK), lambda i, f: (layer, which, 0, f)),
        pl.BlockSpec((None, None, d, FF_CHUNK), lambda i, f: (layer, which, 0, nf + f)),
        pl.BlockSpec((None, None, FF_CHUNK, d), lambda i, f: (layer, which, f, 0)),
    ]
    args = [xs, mod3, norm_g.reshape(1, d), w_in, w_in, w_out]
    if final:
        in_specs.append(pl.BlockSpec((1, d), lambda i, f: (0, 0)))
        args.append(final_g.reshape(1, d))
    return pl.pallas_call(
        functools.partial(_ffn_kernel, n_lat=n_lat, tm=tm, final=final),
        grid=(rows // tm, nf),
        in_specs=in_specs,
        out_specs=pl.BlockSpec((tm, d), lambda i, f: (i, 0)),
        out_shape=jax.ShapeDtypeStruct((rows, d), F32),
        scratch_shapes=[pltpu.VMEM((tm, d), BF16), pltpu.VMEM((tm, d), F32)],
        compiler_params=_cparams(("parallel", "arbitrary")),
        name="ffn",
    )(*args)


def _proj_kernel(x_ref, mod_ref, g_ref, w_ref, cos_ref, sin_ref, qkg_ref, qo_ref, ko_ref, vo_ref, po_ref,
                 h_scr, qkv_scr, *, n_lat, tm):
    i = pl.program_id(0)
    j = pl.program_id(1)

    @pl.when(j == 0)
    def _():
        _norm_mod_store(h_scr, x_ref, mod_ref, g_ref, i * tm, tm, n_lat)
        qkv_scr[...] = jnp.dot(h_scr[...], w_ref[...], preferred_element_type=F32)

    @pl.when(j == 1)
    def _():
        po_ref[...] = jnp.dot(h_scr[...], w_ref[...], preferred_element_type=F32)
        cos = cos_ref[...]
        sin = sin_ref[...]

        def norm_rope(xh, g):
            xh = _rms(xh, g)
            return xh * cos + pltpu.roll(xh, HEAD_DIM // 2, 1) * sin

        for h in range(ATTN_HEADS):
            sl = slice(h * HEAD_DIM, (h + 1) * HEAD_DIM)
            qo_ref[:, sl] = (norm_rope(qkv_scr[:, sl], qkg_ref[0:1, :]) * QK_EXP2_SCALE).astype(BF16)
        for h in range(ATTN_KV_HEADS):
            sl = slice(ATTN_WIDTH + h * HEAD_DIM, ATTN_WIDTH + (h + 1) * HEAD_DIM)
            ko_ref[:, h * HEAD_DIM:(h + 1) * HEAD_DIM] = norm_rope(qkv_scr[:, sl], qkg_ref[1:2, :]).astype(BF16)
        vt = qkv_scr[:, ATTN_WIDTH + KV_WIDTH:].T.astype(BF16)
        ones = jnp.ones((VT_ROWS - HEAD_DIM, tm), BF16)
        for g in range(ATTN_KV_HEADS):
            vo_ref[g * VT_ROWS:g * VT_ROWS + HEAD_DIM, :] = vt[g * HEAD_DIM:(g + 1) * HEAD_DIM, :]
            vo_ref[g * VT_ROWS + HEAD_DIM:(g + 1) * VT_ROWS, :] = ones

    @pl.when(j > 1)
    def _():
        po_ref[...] = jnp.dot(h_scr[...], w_ref[...], preferred_element_type=F32)


def _in_proj(xs, mod3, norm_g, w_in_r, cos, sin, qk_g, *, tm, n_lat):
    n, d = xs.shape
    tn = QKV_COLS
    steps = (QKV_COLS + P_COLS) // tn
    return pl.pallas_call(
        functools.partial(_proj_kernel, n_lat=n_lat, tm=tm),
        grid=(n // tm, steps),
        in_specs=[
            pl.BlockSpec((tm, d), lambda i, j: (i, 0)),
            pl.BlockSpec((3, SUBLANES, d), lambda i, j: (0, 0, 0)),
            pl.BlockSpec((1, d), lambda i, j: (0, 0)),
            pl.BlockSpec((d, tn), lambda i, j: (0, j)),
            pl.BlockSpec((tm, HEAD_DIM), lambda i, j: (i, 0)),
            pl.BlockSpec((tm, HEAD_DIM), lambda i, j: (i, 0)),
            pl.BlockSpec((2, HEAD_DIM), lambda i, j: (0, 0)),
        ],
        out_specs=[
            pl.BlockSpec((tm, ATTN_WIDTH), lambda i, j: (i, 0)),
            pl.BlockSpec((tm, KV_WIDTH), lambda i, j: (i, 0)),
            pl.BlockSpec((ATTN_KV_HEADS * VT_ROWS, tm), lambda i, j: (0, i)),
            pl.BlockSpec((tm, tn), lambda i, j: (i, jnp.maximum(j - 1, 0))),
        ],
        out_shape=[
            jax.ShapeDtypeStruct((n, ATTN_WIDTH), BF16),
            jax.ShapeDtypeStruct((n, KV_WIDTH), BF16),
            jax.ShapeDtypeStruct((ATTN_KV_HEADS * VT_ROWS, n), BF16),
            jax.ShapeDtypeStruct((n, P_COLS), F32),
        ],
        scratch_shapes=[pltpu.VMEM((tm, d), BF16), pltpu.VMEM((tm, tn), F32)],
        compiler_params=_cparams(("parallel", "arbitrary")),
        name="in_proj",
    )(xs, mod3, norm_g.reshape(1, d), w_in_r, cos, sin, qk_g)


def _flash_kernel(*refs, aliased):
    if aliased:
        q_ref, k_ref, vt_ref, _, o_ref, m_scr, acc_scr, *bufs = refs
    else:
        q_ref, k_ref, vt_ref, o_ref, m_scr, acc_scr, *bufs = refs
    s_bufs, p_bufs = bufs[:len(bufs) // 2], bufs[len(bufs) // 2:]
    j = pl.program_id(1)
    tq = q_ref.shape[0]
    tk = k_ref.shape[0]
    w = ATTN_GROUPS * tq
    kc = min(KEY_CHUNK, tk)

    @pl.when(j == 0)
    def _():
        m_scr[...] = jnp.full_like(m_scr, -jnp.inf)
        acc_scr[...] = jnp.zeros_like(acc_scr)

    qs = [jnp.concatenate(
        [q_ref[:, (ATTN_GROUPS * g + h) * HEAD_DIM:(ATTN_GROUPS * g + h + 1) * HEAD_DIM]
         for h in range(ATTN_GROUPS)], axis=0) for g in range(ATTN_KV_HEADS)]
    units = [(g, ci) for ci in range(tk // kc) for g in range(ATTN_KV_HEADS)]

    def scores(u):
        g, ci = units[u]
        st = lax.dot_general(k_ref[ci * kc:(ci + 1) * kc, g * HEAD_DIM:(g + 1) * HEAD_DIM], qs[g],
                             (((1,), (1,)), ((), ())), preferred_element_type=F32)
        s_bufs[u % len(s_bufs)][...] = st
        mx = st[0:SUBLANES]
        for r in range(1, kc // SUBLANES):
            mx = jnp.maximum(mx, st[r * SUBLANES:(r + 1) * SUBLANES])
        return mx

    m_run = [m_scr[g] for g in range(ATTN_KV_HEADS)]
    row0 = jnp.minimum(j, 0)
    pending = [scores(u) for u in range(min(SCORE_AHEAD, len(units)))]
    for u, (g, ci) in enumerate(units):
        mx = pending.pop(0)
        if u + SCORE_AHEAD < len(units):
            pending.append(scores(u + SCORE_AHEAD))
        m_new = jnp.maximum(m_run[g], jnp.max(mx, axis=0, keepdims=True))
        alpha = jnp.exp2(m_run[g] - m_new)
        s_buf, p_buf = s_bufs[u % len(s_bufs)], p_bufs[u % len(p_bufs)]
        for c0 in range(0, w, EXP_LANES):
            cs = slice(c0, min(c0 + EXP_LANES, w))
            m_rep = jnp.broadcast_to(m_new[:, cs], (BF16_ROWS, cs.stop - cs.start))
            for r in range(kc // BF16_ROWS):
                rs = slice(r * BF16_ROWS, (r + 1) * BF16_ROWS)
                rd = pl.ds(pl.multiple_of(row0 + r * BF16_ROWS, BF16_ROWS), BF16_ROWS)
                p_buf[rs, cs] = jnp.exp2(s_buf[rd, cs] - m_rep).astype(BF16)
        pv = jnp.dot(vt_ref[g * VT_ROWS:(g + 1) * VT_ROWS, ci * kc:(ci + 1) * kc], p_buf[...],
                     preferred_element_type=F32)
        acc_scr[g] = alpha * acc_scr[g] + pv
        m_run[g] = m_new
    for g in range(ATTN_KV_HEADS):
        m_scr[g] = m_run[g]

    @pl.when(j == pl.num_programs(1) - 1)
    def _():
        for g in range(ATTN_KV_HEADS):
            acc = acc_scr[g]
            ot = acc[0:HEAD_DIM] / acc[HEAD_DIM:HEAD_DIM + 1]
            for h in range(ATTN_GROUPS):
                col = (ATTN_GROUPS * g + h) * HEAD_DIM
                o_ref[:, col:col + HEAD_DIM] = ot[:, h * tq:(h + 1) * tq].T.astype(o_ref.dtype)


def _attention(q, k, vt, *, q_tiles, q_off, kv_tiles, kv_off, tq, tk, prev=None):
    n = q.shape[0]
    aliased = prev is not None
    in_specs = [
        pl.BlockSpec((tq, ATTN_WIDTH), lambda i, j: (q_off + i, 0)),
        pl.BlockSpec((tk, KV_WIDTH), lambda i, j: (kv_off + j, 0)),
        pl.BlockSpec((ATTN_KV_HEADS * VT_ROWS, tk), lambda i, j: (0, kv_off + j)),
    ]
    args = [q, k, vt]
    if aliased:
        in_specs.append(pl.BlockSpec(memory_space=pl.ANY))
        args.append(prev)
    w = ATTN_GROUPS * tq
    kc = min(KEY_CHUNK, tk)
    units = min(ATTN_KV_HEADS * (tk // kc), SCORE_BUFFERS)
    return pl.pallas_call(
        functools.partial(_flash_kernel, aliased=aliased),
        grid=(q_tiles, kv_tiles),
        in_specs=in_specs,
        out_specs=pl.BlockSpec((tq, ATTN_WIDTH), lambda i, j: (q_off + i, 0)),
        out_shape=jax.ShapeDtypeStruct((n, ATTN_WIDTH), BF16),
        scratch_shapes=[
            pltpu.VMEM((ATTN_KV_HEADS, 1, w), F32),
            pltpu.VMEM((ATTN_KV_HEADS, VT_ROWS, w), F32),
        ] + [pltpu.VMEM((kc, w), F32)] * units + [pltpu.VMEM((kc, w), BF16)] * units,
        input_output_aliases={3: 0} if aliased else {},
        compiler_params=_cparams(("parallel", "arbitrary")),
        name="attention_ctx" if aliased else "attention",
    )(*args)


def _segment_edges(i, lat_tiles):
    first = jnp.logical_or(i == 0, i == lat_tiles)
    last = jnp.logical_or(i == lat_tiles - 1, i == pl.num_programs(0) - 1)
    return first, last


def _ssd_conv_kernel(prev_ref, cur_ref, next_ref, w_ref, b_ref, o_ref, ext_scr, *, lat_tiles):
    i = pl.program_id(0)
    t = cur_ref.shape[0]
    first, last = _segment_edges(i, lat_tiles)
    ext_scr[0:SSD_HALO, :] = jnp.where(first, 0.0, prev_ref[...])
    ext_scr[SSD_HALO:SSD_HALO + t, :] = cur_ref[...]
    ext_scr[SSD_HALO + t:, :] = jnp.where(last, 0.0, next_ref[...])
    pad = SSD_CONV // 2
    acc = jnp.broadcast_to(b_ref[...], cur_ref.shape)
    for k in range(SSD_CONV):
        acc = acc + w_ref[k:k + 1, :] * ext_scr[pl.ds(SSD_HALO - pad + k, t), :]
    o_ref[...] = _silu(acc)


def _ssd_conv(p, conv_w, conv_b, *, n_lat):
    n = p.shape[0]
    t = ROW_TILE
    hb = t // SSD_HALO
    last_hb = n // SSD_HALO - 1
    cb = COL_XBC // SSD_CONV_CH
    return pl.pallas_call(
        functools.partial(_ssd_conv_kernel, lat_tiles=n_lat // t),
        grid=(n // t,),
        in_specs=[
            pl.BlockSpec((SSD_HALO, SSD_CONV_CH), lambda i: (jnp.maximum(i * hb - 1, 0), cb)),
            pl.BlockSpec((t, SSD_CONV_CH), lambda i: (i, cb)),
            pl.BlockSpec((SSD_HALO, SSD_CONV_CH), lambda i: (jnp.minimum((i + 1) * hb, last_hb), cb)),
            pl.BlockSpec((SSD_CONV, SSD_CONV_CH), lambda i: (0, 0)),
            pl.BlockSpec((1, SSD_CONV_CH), lambda i: (0, 0)),
        ],
        out_specs=pl.BlockSpec((t, SSD_CONV_CH), lambda i: (i, 0)),
        out_shape=jax.ShapeDtypeStruct((n, SSD_CONV_CH), F32),
        scratch_shapes=[pltpu.VMEM((t + 2 * SSD_HALO, SSD_CONV_CH), F32)],
        compiler_params=_cparams(("parallel",)),
        name="ssd_conv",
    )(p, p, p, conv_w, conv_b.reshape(1, SSD_CONV_CH))


def _conv_module_tile(ap_ref, gp_ref, a_ref, g_ref, an_ref, gn_ref, w_ref, b_ref, lg_ref, lb_ref, ext_scr,
                      sh_scr, first, last):
    t = a_ref.shape[0]

    def glu(a, gt):
        return a[...] * _sigmoid(gt[...])

    ext_scr[0:CM_HALO, :] = jnp.where(first, 0.0, glu(ap_ref, gp_ref))
    ext_scr[CM_HALO:CM_HALO + t, :] = glu(a_ref, g_ref)
    ext_scr[CM_HALO + t:, :] = jnp.where(last, 0.0, glu(an_ref, gn_ref))
    pad = CM_KERNEL // 2
    first_off = CM_HALO - pad
    span = t + (first_off + CM_KERNEL - 1) // SUBLANES * SUBLANES
    for phase in range(SUBLANES):
        sh_scr[phase] = ext_scr[phase:phase + span, :]
    blocks = []
    for rb in range(0, t, CM_ROWS):
        acc = jnp.broadcast_to(b_ref[...], (CM_ROWS, a_ref.shape[1]))
        for k in range(CM_KERNEL):
            phase = (first_off + k) % SUBLANES
            row = first_off + k - phase + rb
            acc = acc + w_ref[k:k + 1, :] * sh_scr[phase, row:row + CM_ROWS, :]
        mu = jnp.mean(acc, axis=-1, keepdims=True)
        cen = acc - mu
        var = jnp.mean(cen * cen, axis=-1, keepdims=True)
        y = cen * lax.rsqrt(var + EPS) * lg_ref[...] + lb_ref[...]
        blocks.append(_silu(y).astype(BF16))
    return jnp.concatenate(blocks, axis=0)


def _split3_dot(tri, x):
    hi = x.astype(BF16)
    r1 = x - hi.astype(F32)
    mid = r1.astype(BF16)
    lo = (r1 - mid.astype(F32)).astype(BF16)
    return (jnp.dot(tri, hi, preferred_element_type=F32) + jnp.dot(tri, mid, preferred_element_type=F32)
            + jnp.dot(tri, lo, preferred_element_type=F32))


def _ssd_chunk(xa_ref, dt_ref, bias, alog, y_ref, h_scr, backward):
    q = SSD_CHUNK
    x = dt_ref[...] + bias
    dt = jnp.maximum(x, 0.0) + jnp.log1p(jnp.exp(-jnp.abs(x)))
    da = dt * (-jnp.exp(alog))
    ii = lax.broadcasted_iota(jnp.int32, (q, q), 0)
    jj = lax.broadcasted_iota(jnp.int32, (q, q), 1)
    lane = jj
    inc = (jj >= ii) if backward else (jj <= ii)
    tri = jnp.where(inc, 1.0, 0.0).astype(BF16)
    cum = _split3_dot(tri, da)
    tot = jnp.sum(da, axis=0, keepdims=True)
    cum_t = cum.T
    dt_t = dt.T
    e_in = jnp.exp(cum)
    w_out = jnp.exp(tot - cum) * dt
    e_tot = jnp.exp(tot)

    xa = xa_ref[...]
    pair = 2 * SSD_HEAD_DIM
    low = lane < SSD_HEAD_DIM
    for g in range(SSD_GROUPS):
        b = xa[:, SSD_WIDTH + g * SSD_STATE:SSD_WIDTH + (g + 1) * SSD_STATE].astype(BF16)
        cg = xa[:, SSD_WIDTH + (SSD_GROUPS + g) * SSD_STATE:
                SSD_WIDTH + (SSD_GROUPS + g + 1) * SSD_STATE].astype(BF16)
        cb = lax.dot_general(cg, b, (((1,), (1,)), ((), ())), preferred_element_type=F32)
        for m in range(SSD_HEADS // SSD_GROUPS // 2):
            pi = g * (SSD_HEADS // SSD_GROUPS // 2) + m
            h0, h1 = 2 * pi, 2 * pi + 1
            xp = xa[:, pi * pair:(pi + 1) * pair]
            xp_b = xp.astype(BF16)
            ys = []
            for hd in (h0, h1):
                seg = cum[:, hd:hd + 1] - cum_t[hd:hd + 1, :]
                dec = jnp.exp(jnp.where(inc, seg, -jnp.inf))
                sc = (cb * dec * dt_t[hd:hd + 1, :]).astype(BF16)
                ys.append(jnp.dot(sc, xp_b, preferred_element_type=F32))
            y_diag = jnp.where(low, ys[0], ys[1])
            hp = h_scr[pi]
            y_off = lax.dot_general(cg, hp.astype(BF16), (((1,), (1,)), ((), ())),
                                    preferred_element_type=F32)
            y_off = y_off * jnp.where(low, e_in[:, h0:h0 + 1], e_in[:, h1:h1 + 1])
            y_ref[:, pi * pair:(pi + 1) * pair] = y_diag + y_off
            xw = xp * jnp.where(low, w_out[:, h0:h0 + 1], w_out[:, h1:h1 + 1])
            st = jnp.dot(xw.T.astype(BF16), b, preferred_element_type=F32)
            row_low = ii < SSD_HEAD_DIM
            keep = jnp.where(row_low, e_tot[:, h0:h0 + 1], e_tot[:, h1:h1 + 1])
            h_scr[pi] = keep * hp + st


def _ssd_scan_kernel(xaf_ref, xab_ref, dtf_ref, dtb_ref, bias_ref, alog_ref, yf_ref, yb_ref, h_scr):
    @pl.when(pl.program_id(0) == 0)
    def _():
        h_scr[...] = jnp.zeros_like(h_scr)

    _ssd_chunk(xaf_ref, dtf_ref, bias_ref[0], alog_ref[0], yf_ref, h_scr.at[0], False)
    _ssd_chunk(xab_ref, dtb_ref, bias_ref[1], alog_ref[1], yb_ref, h_scr.at[1], True)


def _ssd_scan(xa, p, dt_bias, a_log, *, n_lat):
    n = xa.shape[0]
    nc = n // SSD_CHUNK
    lat_c = n_lat // SSD_CHUNK
    ctx_c = nc - lat_c
    dt_cb = COL_DT // LANES

    def fwd(s):
        return jnp.where(s < ctx_c, lat_c + s, s - ctx_c)

    def bwd(s):
        return nc - 1 - s

    pad = LANES - SSD_HEADS
    bias = jnp.pad(dt_bias, ((0, 0), (0, pad))).reshape(2, 1, LANES)
    alog = jnp.pad(a_log, ((0, 0), (0, pad))).reshape(2, 1, LANES)
    y_sds = jax.ShapeDtypeStruct((n, SSD_WIDTH), F32)
    return pl.pallas_call(
        _ssd_scan_kernel,
        grid=(nc,),
        in_specs=[
            pl.BlockSpec((SSD_CHUNK, SSD_CONV_CH), lambda s: (fwd(s), 0)),
            pl.BlockSpec((SSD_CHUNK, SSD_CONV_CH), lambda s: (bwd(s), 0)),
            pl.BlockSpec((SSD_CHUNK, LANES), lambda s: (fwd(s), dt_cb)),
            pl.BlockSpec((SSD_CHUNK, LANES), lambda s: (bwd(s), dt_cb + 1)),
            pl.BlockSpec((2, 1, LANES), lambda s: (0, 0, 0)),
            pl.BlockSpec((2, 1, LANES), lambda s: (0, 0, 0)),
        ],
        out_specs=[
            pl.BlockSpec((SSD_CHUNK, SSD_WIDTH), lambda s: (fwd(s), 0)),
            pl.BlockSpec((SSD_CHUNK, SSD_WIDTH), lambda s: (bwd(s), 0)),
        ],
        out_shape=[y_sds, y_sds],
        scratch_shapes=[pltpu.VMEM((2, SSD_HEADS // 2, 2 * SSD_HEAD_DIM, SSD_STATE), F32)],
        compiler_params=_cparams(("arbitrary",)),
        name="ssd_scan",
    )(xa, xa, p, p, bias, alog)


def _ssd_out_tile(yf_ref, yb_ref, xs_ref, z_ref, d_ref, g_ref):
    y = yf_ref[...] + yb_ref[...] + d_ref[...] * xs_ref[...]
    y = y * _silu(z_ref[...])
    return _rms(y, g_ref[...]).astype(BF16)


def _mixer_out_kernel(x_ref, attn_ref, yf_ref, yb_ref, xs_ref, z_ref, ap_ref, gp_ref, a_ref, g_ref, an_ref, gn_ref,
                      mod_ref, w_ref, d_ref, sg_ref, cw_ref, cb_ref, lg_ref, lb_ref, o_ref, ext_scr, sh_scr,
                      acc_scr, *, n_lat, lat_tiles):
    i = pl.program_id(0)
    t = x_ref.shape[0]
    first, last = _segment_edges(i, lat_tiles)
    acc_scr[...] = jnp.dot(attn_ref[...], w_ref[0:ATTN_WIDTH, :], preferred_element_type=F32)
    ssd = _ssd_out_tile(yf_ref, yb_ref, xs_ref, z_ref, d_ref, sg_ref)
    cm = _conv_module_tile(ap_ref, gp_ref, a_ref, g_ref, an_ref, gn_ref, cw_ref, cb_ref, lg_ref, lb_ref,
                           ext_scr, sh_scr, first, last)
    rest = jnp.dot(ssd, w_ref[ATTN_WIDTH:ATTN_WIDTH + SSD_WIDTH, :], preferred_element_type=F32)
    rest += jnp.dot(cm, w_ref[ATTN_WIDTH + SSD_WIDTH:, :], preferred_element_type=F32)
    rows = pl.ds(pl.multiple_of(jnp.minimum(i, 0), SUBLANES), t)
    o_ref[...] = x_ref[...] + _mod_row(mod_ref, 2, i * t, n_lat) * (acc_scr[rows, :] + rest)


def _mixer_out(xs, attn, y_f, y_b, xa, p, mod3, w_out, d_skip, ssd_norm_g, dw_w, dw_b, ln_g, ln_b, *, rows, n_lat):
    d = xs.shape[1]
    n = p.shape[0]
    t = ROW_TILE
    hb = t // CM_HALO
    last_hb = n // CM_HALO - 1
    ca = COL_GLU // CM_WIDTH
    cg = ca + 1

    def prev_map(c):
        return lambda i: (jnp.maximum(i * hb - 1, 0), c)

    def next_map(c):
        return lambda i: (jnp.minimum((i + 1) * hb, last_hb), c)

    def row(width, col=0):
        return pl.BlockSpec((t, width), lambda i: (i, col))

    vec = pl.BlockSpec((1, CM_WIDTH), lambda i: (0, 0))
    return pl.pallas_call(
        functools.partial(_mixer_out_kernel, n_lat=n_lat, lat_tiles=n_lat // t),
        grid=(rows // t,),
        in_specs=[
            row(d), row(ATTN_WIDTH),
            row(SSD_WIDTH), row(SSD_WIDTH), row(SSD_WIDTH), row(SSD_WIDTH, COL_Z // SSD_WIDTH),
            pl.BlockSpec((CM_HALO, CM_WIDTH), prev_map(ca)),
            pl.BlockSpec((CM_HALO, CM_WIDTH), prev_map(cg)),
            row(CM_WIDTH, ca), row(CM_WIDTH, cg),
            pl.BlockSpec((CM_HALO, CM_WIDTH), next_map(ca)),
            pl.BlockSpec((CM_HALO, CM_WIDTH), next_map(cg)),
            pl.BlockSpec((3, SUBLANES, d), lambda i: (0, 0, 0)),
            pl.BlockSpec((D_MIX, d), lambda i: (0, 0)),
            vec, vec,
            pl.BlockSpec((CM_KERNEL, CM_WIDTH), lambda i: (0, 0)),
            vec, vec, vec,
        ],
        out_specs=pl.BlockSpec((t, d), lambda i: (i, 0)),
        out_shape=jax.ShapeDtypeStruct((rows, d), F32),
        scratch_shapes=[pltpu.VMEM((t + 2 * CM_HALO, CM_WIDTH), F32),
                        pltpu.VMEM((SUBLANES, t + 2 * CM_HALO - SUBLANES, CM_WIDTH), F32),
                        pltpu.VMEM((t, d), F32)],
        compiler_params=_cparams(("parallel",)),
        name="mixer_out",
    )(xs, attn, y_f, y_b, xa, p, p, p, p, p, p, p, mod3, w_out,
      jnp.repeat(d_skip, SSD_HEAD_DIM).reshape(1, SSD_WIDTH), ssd_norm_g.reshape(1, SSD_WIDTH),
      dw_w, dw_b.reshape(1, -1), ln_g.reshape(1, -1), ln_b.reshape(1, -1))


def _rope_tables(n_lat, n):
    rows = n_lat // GRID_W
    row = jnp.repeat(jnp.arange(rows), GRID_W).astype(F32)
    col = jnp.tile(jnp.arange(GRID_W), rows).astype(F32)
    inv = ROPE_THETA ** (-jnp.arange(0, ROPE_AXIS_DIM, 2, dtype=F32) / ROPE_AXIS_DIM)
    ar = row[:, None] * inv
    ac = col[:, None] * inv
    cos = jnp.concatenate([jnp.cos(ar), jnp.cos(ac), jnp.cos(ar), jnp.cos(ac)], axis=1)
    sin = jnp.concatenate([-jnp.sin(ar), -jnp.sin(ac), jnp.sin(ar), jnp.sin(ac)], axis=1)
    pad = n - n_lat
    cos = jnp.concatenate([cos, jnp.ones((pad, HEAD_DIM), F32)], axis=0)
    sin = jnp.concatenate([sin, jnp.zeros((pad, HEAD_DIM), F32)], axis=0)
    return cos, sin


def _permute_head_dims(a):
    lead = a.shape[:-1]
    quarter = ROPE_AXIS_DIM // 2
    a = a.reshape(*lead, -1, 2, 2, quarter)
    return jnp.swapaxes(a, -3, -2).reshape(*lead, -1)


def _relayout_w_in(w_in):
    d = w_in.shape[0]
    o_k = ATTN_WIDTH
    o_v = o_k + KV_WIDTH
    o_z = o_v + KV_WIDTH
    o_xbc = o_z + SSD_WIDTH
    o_dt = o_xbc + SSD_CONV_CH
    o_glu = o_dt + 2 * SSD_HEADS
    dt = w_in[:, o_dt:o_glu]
    zpad = jnp.zeros((d, LANES - SSD_HEADS), w_in.dtype)
    tail = jnp.zeros((d, P_COLS - COL_DT - 2 * LANES), w_in.dtype)
    return jnp.concatenate([
        _permute_head_dims(w_in[:, :o_k]), _permute_head_dims(w_in[:, o_k:o_v]), w_in[:, o_v:o_z],
        w_in[:, o_xbc:o_dt], w_in[:, o_z:o_xbc], w_in[:, o_glu:],
        dt[:, :SSD_HEADS], zpad, dt[:, SSD_HEADS:], zpad, tail], axis=1).astype(BF16)


def kernel(x, c, ctx, c_ctx, w_mod, b_mod, norm_g, w_ffn_in, w_ffn_out, w_in, w_out, qk_g,
           ssd_conv_w, ssd_conv_b, ssd_dt_bias, ssd_a_log, ssd_d, ssd_norm_g,
           cm_dw_w, cm_dw_b, cm_ln_g, cm_ln_b, final_g):
    assert x.shape[0] == 1 and ctx.shape[0] == 1
    n_lat, d = x.shape[1], x.shape[2]
    n_ctx = ctx.shape[1]
    n = n_lat + n_ctx
    depth = w_mod.shape[0]
    assert n % MM_TILE == 0 and n_lat % ROW_TILE == 0 and n_ctx % ROW_TILE == 0
    assert ROW_TILE % ROW_CHUNK == 0 and MM_TILE % (2 * ROW_CHUNK) == 0
    kv_tile = max(t for t in range(KEY_CHUNK, KV_TILE + 1, KEY_CHUNK) if n % t == 0)
    lat_tm = 512
    assert n_lat % lat_tm == 0 and n_lat % GRID_W == 0

    xs = jnp.concatenate([x[0], ctx[0]], axis=0)
    cvec = jnp.concatenate([c_ctx[None, :], c, jnp.zeros((SUBLANES - 2, d), F32)], axis=0)
    mods = _modulation(cvec, w_mod, b_mod)
    mods = mods.reshape(depth, SUBLANES, N_MOD, d).transpose(0, 2, 1, 3)
    cos, sin = _rope_tables(n_lat, n)
    w_ffn_in_b = w_ffn_in.astype(BF16)
    w_ffn_out_b = w_ffn_out.astype(BF16)
    w_out_b = w_out.astype(BF16)

    out = None
    for l in range(depth):
        last = l == depth - 1
        m = mods[l]
        xs = _ffn(xs, m[0:3], norm_g[l, 0], w_ffn_in_b, w_ffn_out_b, l, 0,
                  rows=n, tm=MM_TILE, n_lat=n_lat)
        q, k, v, p = _in_proj(xs, m[3:6], norm_g[l, 1], _relayout_w_in(w_in[l]), cos, sin,
                              _permute_head_dims(qk_g[l]), tm=MM_TILE, n_lat=n_lat)
        attn = _attention(q, k, v, q_tiles=n_lat // Q_TILE, q_off=0, kv_tiles=n // kv_tile, kv_off=0,
                          tq=Q_TILE, tk=kv_tile)
        if not last:
            attn = _attention(q, k, v, q_tiles=n_ctx // ROW_TILE, q_off=n_lat // ROW_TILE,
                              kv_tiles=n_ctx // ROW_TILE, kv_off=n_lat // ROW_TILE,
                              tq=ROW_TILE, tk=ROW_TILE, prev=attn)
        xa = _ssd_conv(p, ssd_conv_w[l], ssd_conv_b[l], n_lat=n_lat)
        y_f, y_b = _ssd_scan(xa, p, ssd_dt_bias[l], ssd_a_log[l], n_lat=n_lat)
        xs = _mixer_out(xs, attn, y_f, y_b, xa, p, m[3:6], w_out_b[l], ssd_d[l], ssd_norm_g[l],
                        cm_dw_w[l], cm_dw_b[l], cm_ln_g[l], cm_ln_b[l], rows=n_lat if last else n, n_lat=n_lat)
        if last:
            out = _ffn(xs, m[6:9], norm_g[l, 2], w_ffn_in_b, w_ffn_out_b, l, 1,
                       rows=n_lat, tm=lat_tm, n_lat=n_lat, final_g=final_g)
        else:
            xs = _ffn(xs, m[6:9], norm_g[l, 2], w_ffn_in_b, w_ffn_out_b, l, 1,
                      rows=n, tm=MM_TILE, n_lat=n_lat)
    return out[None]
```

```python
import functools
import math

import jax
import jax.numpy as jnp
from jax import lax
from jax.experimental import pallas as pl
from jax.experimental.pallas import tpu as pltpu

F32 = jnp.float32
BF16 = jnp.bfloat16

D_MODEL = 2048
DEPTH = 2
GRID_W = 64
N_MOD = 9
D_FF = 5632
EPS = 1e-6

HEAD_DIM = 128
ATTN_HEADS = 8
ATTN_KV_HEADS = 2
ATTN_GROUPS = ATTN_HEADS // ATTN_KV_HEADS
ATTN_WIDTH = ATTN_HEADS * HEAD_DIM
KV_WIDTH = ATTN_KV_HEADS * HEAD_DIM
ROPE_THETA = 10000.0
ROPE_AXIS_DIM = HEAD_DIM // 2

SSD_HEADS = 8
SSD_HEAD_DIM = 64
SSD_WIDTH = SSD_HEADS * SSD_HEAD_DIM
SSD_GROUPS = 2
SSD_STATE = 128
SSD_CONV = 7
SSD_CHUNK = 128
SSD_CONV_CH = SSD_WIDTH + 2 * SSD_GROUPS * SSD_STATE

CM_WIDTH = 512
CM_KERNEL = 31
D_MIX = ATTN_WIDTH + SSD_WIDTH + CM_WIDTH

LANES = 128
SUBLANES = 8
VMEM_LIMIT = 56 * 1024 * 1024

QKV_COLS = ATTN_WIDTH + 2 * KV_WIDTH
COL_XBC = 0
COL_Z = COL_XBC + SSD_CONV_CH
COL_GLU = COL_Z + SSD_WIDTH
COL_DT = COL_GLU + 2 * CM_WIDTH
P_COLS = 2 * QKV_COLS

ROW_TILE = 256
MM_TILE = 640
FF_CHUNK = 512
KV_TILE = 3328
Q_TILE = 256
EXP_LANES = 1024
SSD_HALO = 8
CM_HALO = 16
CM_ROWS = 64
BF16_ROWS = 16
VT_ROWS = HEAD_DIM + BF16_ROWS
QK_EXP2_SCALE = (HEAD_DIM ** -0.5) * math.log2(math.e)
KEY_CHUNK = 256
SCORE_BUFFERS = 3
SCORE_AHEAD = 1


def _cparams(sem):
    return pltpu.CompilerParams(dimension_semantics=sem, vmem_limit_bytes=VMEM_LIMIT)


def _sigmoid(x):
    return 1.0 / (1.0 + jnp.exp(-x))


def _silu(x):
    return x * _sigmoid(x)


def _rms(x, g):
    return x * lax.rsqrt(jnp.mean(x * x, axis=-1, keepdims=True) + EPS) * g


ROW_CHUNK = 64


def _mod_row(mod_ref, k, row0, n_lat):
    cls = (row0 < n_lat).astype(jnp.int32)
    return mod_ref[k, pl.ds(cls, 1), :]


def _norm_mod_store(h_scr, x_ref, mod_ref, g_ref, row0, tm, n_lat, start=0):
    for r in range(start, tm, ROW_CHUNK):
        shift = _mod_row(mod_ref, 0, row0 + r, n_lat)
        gain = g_ref[...] * (1.0 + _mod_row(mod_ref, 1, row0 + r, n_lat))
        x = x_ref[r:r + ROW_CHUNK, :]
        inv = lax.rsqrt(jnp.mean(x * x, axis=-1, keepdims=True) + EPS)
        h_scr[r:r + ROW_CHUNK, :] = ((x * inv) * gain + shift).astype(BF16)


def _mod_kernel(c_ref, w_ref, b_ref, o_ref):
    a = _silu(c_ref[...]).astype(BF16)
    o_ref[0] = jnp.dot(a, w_ref[0].astype(BF16), preferred_element_type=F32) + b_ref[0]


def _modulation(cvec, w_mod, b_mod):
    depth, d, n = w_mod.shape
    tn = 1024
    return pl.pallas_call(
        _mod_kernel,
        grid=(depth, n // tn),
        in_specs=[
            pl.BlockSpec((SUBLANES, d), lambda l, j: (0, 0)),
            pl.BlockSpec((1, d, tn), lambda l, j: (l, 0, j)),
            pl.BlockSpec((1, 1, tn), lambda l, j: (l, 0, j)),
        ],
        out_specs=pl.BlockSpec((1, SUBLANES, tn), lambda l, j: (l, 0, j)),
        out_shape=jax.ShapeDtypeStruct((depth, SUBLANES, n), F32),
        compiler_params=_cparams(("arbitrary", "arbitrary")),
        name="modulation",
    )(cvec, w_mod, b_mod.reshape(depth, 1, n))


def _ffn_kernel(x_ref, mod_ref, g_ref, wg_ref, wu_ref, wo_ref, *rest, n_lat, tm, final):
    if final:
        fg_ref, o_ref, h_scr, acc_scr = rest
    else:
        o_ref, h_scr, acc_scr = rest
    i = pl.program_id(0)
    f = pl.program_id(1)
    last = pl.num_programs(1) - 1
    half = tm // 2

    def swiglu_chunk(rows, first):
        h = h_scr[rows, :]
        gt = jnp.dot(h, wg_ref[...], preferred_element_type=F32)
        up = jnp.dot(h, wu_ref[...], preferred_element_type=F32)
        out = jnp.dot((_silu(gt) * up).astype(BF16), wo_ref[...], preferred_element_type=F32)
        if first:
            acc_scr[rows, :] = out
        else:
            acc_scr[rows, :] += out

    def finish(start, stop):
        for r in range(start, stop, ROW_CHUNK):
            rs = slice(r, r + ROW_CHUNK)
            gate = _mod_row(mod_ref, 2, i * tm + r, n_lat)
            y = x_ref[rs, :] + (0.5 * gate) * acc_scr[rs, :]
            if final:
                y = _rms(y, fg_ref[...])
            o_ref[rs, :] = y

    @pl.when(f == 0)
    def _():
        _norm_mod_store(h_scr, x_ref, mod_ref, g_ref, i * tm, half, n_lat)
        swiglu_chunk(slice(0, half), True)
        _norm_mod_store(h_scr, x_ref, mod_ref, g_ref, i * tm, tm, n_lat, start=half)
        swiglu_chunk(slice(half, tm), True)

    @pl.when(jnp.logical_and(f > 0, f < last))
    def _():
        swiglu_chunk(slice(0, tm), False)

    @pl.when(f == last)
    def _():
        swiglu_chunk(slice(0, half), False)
        finish(0, half)
        swiglu_chunk(slice(half, tm), False)
        finish(half, tm)


def _ffn(xs, mod3, norm_g, w_in, w_out, layer, which, *, rows, tm, n_lat, final_g=None):
    d = xs.shape[1]
    nf = D_FF // FF_CHUNK
    final = final_g is not None
    in_specs = [
        pl.BlockSpec((tm, d), lambda i, f: (i, 0)),
        pl.BlockSpec((3, SUBLANES, d), lambda i, f: (0, 0, 0)),
        pl.BlockSpec((1, d), lambda i, f: (0, 0)),
        pl.BlockSpec((None, None, d, FF_CHUNK), lambda i, f: (layer, which, 0, f)),
        pl.BlockSpec((None, None, d, FF_CHUNK), lambda i, f: (layer, which, 0, nf + f)),
        pl.BlockSpec((None, None, FF_CHUNK, d), lambda i, f: (layer, which, f, 0)),
    ]
    args = [xs, mod3, norm_g.reshape(1, d), w_in, w_in, w_out]
    if final:
        in_specs.append(pl.BlockSpec((1, d), lambda i, f: (0, 0)))
        args.append(final_g.reshape(1, d))
    return pl.pallas_call(
        functools.partial(_ffn_kernel, n_lat=n_lat, tm=tm, final=final),
        grid=(rows // tm, nf),
        in_specs=in_specs,
        out_specs=pl.BlockSpec((tm, d), lambda i, f: (i, 0)),
        out_shape=jax.ShapeDtypeStruct((rows, d), F32),
        scratch_shapes=[pltpu.VMEM((tm, d), BF16), pltpu.VMEM((tm, d), F32)],
        compiler_params=_cparams(("parallel", "arbitrary")),
        name="ffn",
    )(*args)


def _proj_kernel(x_ref, mod_ref, g_ref, w_ref, cos_ref, sin_ref, qkg_ref, qo_ref, ko_ref, vo_ref, po_ref,
                 h_scr, qkv_scr, *, n_lat, tm):
    i = pl.program_id(0)
    j = pl.program_id(1)

    @pl.when(j == 0)
    def _():
        _norm_mod_store(h_scr, x_ref, mod_ref, g_ref, i * tm, tm, n_lat)
        qkv_scr[...] = jnp.dot(h_scr[...], w_ref[...], preferred_element_type=F32)

    @pl.when(j == 1)
    def _():
        po_ref[...] = jnp.dot(h_scr[...], w_ref[...], preferred_element_type=F32)
        cos = cos_ref[...]
        sin = sin_ref[...]

        def norm_rope(xh, g):
            xh = _rms(xh, g)
            return xh * cos + pltpu.roll(xh, HEAD_DIM // 2, 1) * sin

        for h in range(ATTN_HEADS):
            sl = slice(h * HEAD_DIM, (h + 1) * HEAD_DIM)
            qh = norm_rope(qkv_scr[:, sl], qkg_ref[0:1, :]) * QK_EXP2_SCALE
            qo_ref[sl, :] = qh.T.astype(BF16)
        for h in range(ATTN_KV_HEADS):
            sl = slice(ATTN_WIDTH + h * HEAD_DIM, ATTN_WIDTH + (h + 1) * HEAD_DIM)
            ko_ref[:, h * HEAD_DIM:(h + 1) * HEAD_DIM] = norm_rope(qkv_scr[:, sl], qkg_ref[1:2, :]).astype(BF16)
        vt = qkv_scr[:, ATTN_WIDTH + KV_WIDTH:].T.astype(BF16)
        ones = jnp.ones((VT_ROWS - HEAD_DIM, tm), BF16)
        for g in range(ATTN_KV_HEADS):
            vo_ref[g * VT_ROWS:g * VT_ROWS + HEAD_DIM, :] = vt[g * HEAD_DIM:(g + 1) * HEAD_DIM, :]
            vo_ref[g * VT_ROWS + HEAD_DIM:(g + 1) * VT_ROWS, :] = ones

    @pl.when(j > 1)
    def _():
        po_ref[...] = jnp.dot(h_scr[...], w_ref[...], preferred_element_type=F32)


def _in_proj(xs, mod3, norm_g, w_in_r, cos, sin, qk_g, *, tm, n_lat):
    n, d = xs.shape
    tn = QKV_COLS
    steps = (QKV_COLS + P_COLS) // tn
    return pl.pallas_call(
        functools.partial(_proj_kernel, n_lat=n_lat, tm=tm),
        grid=(n // tm, steps),
        in_specs=[
            pl.BlockSpec((tm, d), lambda i, j: (i, 0)),
            pl.BlockSpec((3, SUBLANES, d), lambda i, j: (0, 0, 0)),
            pl.BlockSpec((1, d), lambda i, j: (0, 0)),
            pl.BlockSpec((d, tn), lambda i, j: (0, j)),
            pl.BlockSpec((tm, HEAD_DIM), lambda i, j: (i, 0)),
            pl.BlockSpec((tm, HEAD_DIM), lambda i, j: (i, 0)),
            pl.BlockSpec((2, HEAD_DIM), lambda i, j: (0, 0)),
        ],
        out_specs=[
            pl.BlockSpec((ATTN_WIDTH, tm), lambda i, j: (0, i)),
            pl.BlockSpec((tm, KV_WIDTH), lambda i, j: (i, 0)),
            pl.BlockSpec((ATTN_KV_HEADS * VT_ROWS, tm), lambda i, j: (0, i)),
            pl.BlockSpec((tm, tn), lambda i, j: (i, jnp.maximum(j - 1, 0))),
        ],
        out_shape=[
            jax.ShapeDtypeStruct((ATTN_WIDTH, n), BF16),
            jax.ShapeDtypeStruct((n, KV_WIDTH), BF16),
            jax.ShapeDtypeStruct((ATTN_KV_HEADS * VT_ROWS, n), BF16),
            jax.ShapeDtypeStruct((n, P_COLS), F32),
        ],
        scratch_shapes=[pltpu.VMEM((tm, d), BF16), pltpu.VMEM((tm, tn), F32)],
        compiler_params=_cparams(("parallel", "arbitrary")),
        name="in_proj",
    )(xs, mod3, norm_g.reshape(1, d), w_in_r, cos, sin, qk_g)


def _flash_kernel(*refs, aliased):
    if aliased:
        qt_ref, k_ref, vt_ref, _, o_ref, m_scr, acc_scr, *bufs = refs
    else:
        qt_ref, k_ref, vt_ref, o_ref, m_scr, acc_scr, *bufs = refs
    s_bufs, p_bufs = bufs[:len(bufs) // 2], bufs[len(bufs) // 2:]
    j = pl.program_id(1)
    tq = qt_ref.shape[1]
    tk = k_ref.shape[0]
    w = ATTN_GROUPS * tq
    kc = min(KEY_CHUNK, tk)

    @pl.when(j == 0)
    def _():
        m_scr[...] = jnp.full_like(m_scr, -jnp.inf)
        acc_scr[...] = jnp.zeros_like(acc_scr)

    qs = [jnp.concatenate(
        [qt_ref[(ATTN_GROUPS * g + h) * HEAD_DIM:(ATTN_GROUPS * g + h + 1) * HEAD_DIM, :]
         for h in range(ATTN_GROUPS)], axis=1) for g in range(ATTN_KV_HEADS)]
    units = [(g, ci) for ci in range(tk // kc) for g in range(ATTN_KV_HEADS)]

    def scores(u):
        g, ci = units[u]
        st = jnp.dot(k_ref[ci * kc:(ci + 1) * kc, g * HEAD_DIM:(g + 1) * HEAD_DIM], qs[g],
                     preferred_element_type=F32)
        s_bufs[u % len(s_bufs)][...] = st
        mx = st[0:SUBLANES]
        for r in range(1, kc // SUBLANES):
            mx = jnp.maximum(mx, st[r * SUBLANES:(r + 1) * SUBLANES])
        return mx

    m_run = [m_scr[g] for g in range(ATTN_KV_HEADS)]
    row0 = jnp.minimum(j, 0)
    pending = [scores(u) for u in range(min(SCORE_AHEAD, len(units)))]
    for u, (g, ci) in enumerate(units):
        mx = pending.pop(0)
        if u + SCORE_AHEAD < len(units):
            pending.append(scores(u + SCORE_AHEAD))
        m_new = jnp.maximum(m_run[g], jnp.max(mx, axis=0, keepdims=True))
        alpha = jnp.exp2(m_run[g] - m_new)
        s_buf, p_buf = s_bufs[u % len(s_bufs)], p_bufs[u % len(p_bufs)]
        for c0 in range(0, w, EXP_LANES):
            cs = slice(c0, min(c0 + EXP_LANES, w))
            m_rep = jnp.broadcast_to(m_new[:, cs], (BF16_ROWS, cs.stop - cs.start))
            for r in range(kc // BF16_ROWS):
                rs = slice(r * BF16_ROWS, (r + 1) * BF16_ROWS)
                rd = pl.ds(pl.multiple_of(row0 + r * BF16_ROWS, BF16_ROWS), BF16_ROWS)
                p_buf[rs, cs] = jnp.exp2(s_buf[rd, cs] - m_rep).astype(BF16)
        pv = jnp.dot(vt_ref[g * VT_ROWS:(g + 1) * VT_ROWS, ci * kc:(ci + 1) * kc], p_buf[...],
                     preferred_element_type=F32)
        acc_scr[g] = alpha * acc_scr[g] + pv
        m_run[g] = m_new
    for g in range(ATTN_KV_HEADS):
        m_scr[g] = m_run[g]

    @pl.when(j == pl.num_programs(1) - 1)
    def _():
        for g in range(ATTN_KV_HEADS):
            acc = acc_scr[g]
            ot = acc[0:HEAD_DIM] / acc[HEAD_DIM:HEAD_DIM + 1]
            for h in range(ATTN_GROUPS):
                col = (ATTN_GROUPS * g + h) * HEAD_DIM
                o_ref[:, col:col + HEAD_DIM] = ot[:, h * tq:(h + 1) * tq].T.astype(o_ref.dtype)


def _attention(qt, k, vt, *, q_tiles, q_off, kv_tiles, kv_off, tq, tk, prev=None):
    n = qt.shape[1]
    aliased = prev is not None
    in_specs = [
        pl.BlockSpec((ATTN_WIDTH, tq), lambda i, j: (0, q_off + i)),
        pl.BlockSpec((tk, KV_WIDTH), lambda i, j: (kv_off + j, 0)),
        pl.BlockSpec((ATTN_KV_HEADS * VT_ROWS, tk), lambda i, j: (0, kv_off + j)),
    ]
    args = [qt, k, vt]
    if aliased:
        in_specs.append(pl.BlockSpec(memory_space=pl.ANY))
        args.append(prev)
    w = ATTN_GROUPS * tq
    kc = min(KEY_CHUNK, tk)
    units = min(ATTN_KV_HEADS * (tk // kc), SCORE_BUFFERS)
    return pl.pallas_call(
        functools.partial(_flash_kernel, aliased=aliased),
        grid=(q_tiles, kv_tiles),
        in_specs=in_specs,
        out_specs=pl.BlockSpec((tq, ATTN_WIDTH), lambda i, j: (q_off + i, 0)),
        out_shape=jax.ShapeDtypeStruct((n, ATTN_WIDTH), BF16),
        scratch_shapes=[
            pltpu.VMEM((ATTN_KV_HEADS, 1, w), F32),
            pltpu.VMEM((ATTN_KV_HEADS, VT_ROWS, w), F32),
        ] + [pltpu.VMEM((kc, w), F32)] * units + [pltpu.VMEM((kc, w), BF16)] * units,
        input_output_aliases={3: 0} if aliased else {},
        compiler_params=_cparams(("parallel", "arbitrary")),
        name="attention_ctx" if aliased else "attention",
    )(*args)


def _segment_edges(i, lat_tiles):
    first = jnp.logical_or(i == 0, i == lat_tiles)
    last = jnp.logical_or(i == lat_tiles - 1, i == pl.num_programs(0) - 1)
    return first, last


def _ssd_conv_kernel(prev_ref, cur_ref, next_ref, w_ref, b_ref, o_ref, ext_scr, *, lat_tiles):
    i = pl.program_id(0)
    t = cur_ref.shape[0]
    first, last = _segment_edges(i, lat_tiles)
    ext_scr[0:SSD_HALO, :] = jnp.where(first, 0.0, prev_ref[...])
    ext_scr[SSD_HALO:SSD_HALO + t, :] = cur_ref[...]
    ext_scr[SSD_HALO + t:, :] = jnp.where(last, 0.0, next_ref[...])
    pad = SSD_CONV // 2
    acc = jnp.broadcast_to(b_ref[...], cur_ref.shape)
    for k in range(SSD_CONV):
        acc = acc + w_ref[k:k + 1, :] * ext_scr[pl.ds(SSD_HALO - pad + k, t), :]
    o_ref[...] = _silu(acc)


def _ssd_conv(p, conv_w, conv_b, *, n_lat):
    n = p.shape[0]
    t = ROW_TILE
    hb = t // SSD_HALO
    last_hb = n // SSD_HALO - 1
    cb = COL_XBC // SSD_CONV_CH
    return pl.pallas_call(
        functools.partial(_ssd_conv_kernel, lat_tiles=n_lat // t),
        grid=(n // t,),
        in_specs=[
            pl.BlockSpec((SSD_HALO, SSD_CONV_CH), lambda i: (jnp.maximum(i * hb - 1, 0), cb)),
            pl.BlockSpec((t, SSD_CONV_CH), lambda i: (i, cb)),
            pl.BlockSpec((SSD_HALO, SSD_CONV_CH), lambda i: (jnp.minimum((i + 1) * hb, last_hb), cb)),
            pl.BlockSpec((SSD_CONV, SSD_CONV_CH), lambda i: (0, 0)),
            pl.BlockSpec((1, SSD_CONV_CH), lambda i: (0, 0)),
        ],
        out_specs=pl.BlockSpec((t, SSD_CONV_CH), lambda i: (i, 0)),
        out_shape=jax.ShapeDtypeStruct((n, SSD_CONV_CH), F32),
        scratch_shapes=[pltpu.VMEM((t + 2 * SSD_HALO, SSD_CONV_CH), F32)],
        compiler_params=_cparams(("parallel",)),
        name="ssd_conv",
    )(p, p, p, conv_w, conv_b.reshape(1, SSD_CONV_CH))


def _conv_module_tile(ap_ref, gp_ref, a_ref, g_ref, an_ref, gn_ref, w_ref, b_ref, lg_ref, lb_ref, ext_scr,
                      sh_scr, first, last):
    t = a_ref.shape[0]

    def glu(a, gt):
        return a[...] * _sigmoid(gt[...])

    ext_scr[0:CM_HALO, :] = jnp.where(first, 0.0, glu(ap_ref, gp_ref))
    ext_scr[CM_HALO:CM_HALO + t, :] = glu(a_ref, g_ref)
    ext_scr[CM_HALO + t:, :] = jnp.where(last, 0.0, glu(an_ref, gn_ref))
    pad = CM_KERNEL // 2
    first_off = CM_HALO - pad
    span = t + (first_off + CM_KERNEL - 1) // SUBLANES * SUBLANES
    for phase in range(SUBLANES):
        sh_scr[phase] = ext_scr[phase:phase + span, :]
    blocks = []
    for rb in range(0, t, CM_ROWS):
        acc = jnp.broadcast_to(b_ref[...], (CM_ROWS, a_ref.shape[1]))
        for k in range(CM_KERNEL):
            phase = (first_off + k) % SUBLANES
            row = first_off + k - phase + rb
            acc = acc + w_ref[k:k + 1, :] * sh_scr[phase, row:row + CM_ROWS, :]
        mu = jnp.mean(acc, axis=-1, keepdims=True)
        cen = acc - mu
        var = jnp.mean(cen * cen, axis=-1, keepdims=True)
        y = cen * lax.rsqrt(var + EPS) * lg_ref[...] + lb_ref[...]
        blocks.append(_silu(y).astype(BF16))
    return jnp.concatenate(blocks, axis=0)


def _split3_dot(tri, x):
    hi = x.astype(BF16)
    r1 = x - hi.astype(F32)
    mid = r1.astype(BF16)
    lo = (r1 - mid.astype(F32)).astype(BF16)
    return (jnp.dot(tri, hi, preferred_element_type=F32) + jnp.dot(tri, mid, preferred_element_type=F32)
            + jnp.dot(tri, lo, preferred_element_type=F32))


def _ssd_chunk(xa_ref, dt_ref, bias, alog, y_ref, h_scr, backward):
    q = SSD_CHUNK
    x = dt_ref[...] + bias
    dt = jnp.maximum(x, 0.0) + jnp.log1p(jnp.exp(-jnp.abs(x)))
    da = dt * (-jnp.exp(alog))
    ii = lax.broadcasted_iota(jnp.int32, (q, q), 0)
    jj = lax.broadcasted_iota(jnp.int32, (q, q), 1)
    lane = jj
    inc = (jj >= ii) if backward else (jj <= ii)
    tri = jnp.where(inc, 1.0, 0.0).astype(BF16)
    cum = _split3_dot(tri, da)
    tot = jnp.sum(da, axis=0, keepdims=True)
    cum_t = cum.T
    dt_t = dt.T
    e_in = jnp.exp(cum)
    w_out = jnp.exp(tot - cum) * dt
    e_tot = jnp.exp(tot)

    xa = xa_ref[...]
    pair = 2 * SSD_HEAD_DIM
    low = lane < SSD_HEAD_DIM
    for g in range(SSD_GROUPS):
        b = xa[:, SSD_WIDTH + g * SSD_STATE:SSD_WIDTH + (g + 1) * SSD_STATE].astype(BF16)
        cg = xa[:, SSD_WIDTH + (SSD_GROUPS + g) * SSD_STATE:
                SSD_WIDTH + (SSD_GROUPS + g + 1) * SSD_STATE].astype(BF16)
        cb = lax.dot_general(cg, b, (((1,), (1,)), ((), ())), preferred_element_type=F32)
        for m in range(SSD_HEADS // SSD_GROUPS // 2):
            pi = g * (SSD_HEADS // SSD_GROUPS // 2) + m
            h0, h1 = 2 * pi, 2 * pi + 1
            xp = xa[:, pi * pair:(pi + 1) * pair]
            xp_b = xp.astype(BF16)
            ys = []
            for hd in (h0, h1):
                seg = cum[:, hd:hd + 1] - cum_t[hd:hd + 1, :]
                dec = jnp.exp(jnp.where(inc, seg, -jnp.inf))
                sc = (cb * dec * dt_t[hd:hd + 1, :]).astype(BF16)
                ys.append(jnp.dot(sc, xp_b, preferred_element_type=F32))
            y_diag = jnp.where(low, ys[0], ys[1])
            hp = h_scr[pi]
            y_off = lax.dot_general(cg, hp.astype(BF16), (((1,), (1,)), ((), ())),
                                    preferred_element_type=F32)
            y_off = y_off * jnp.where(low, e_in[:, h0:h0 + 1], e_in[:, h1:h1 + 1])
            y_ref[:, pi * pair:(pi + 1) * pair] = y_diag + y_off
            xw = xp * jnp.where(low, w_out[:, h0:h0 + 1], w_out[:, h1:h1 + 1])
            st = jnp.dot(xw.T.astype(BF16), b, preferred_element_type=F32)
            row_low = ii < SSD_HEAD_DIM
            keep = jnp.where(row_low, e_tot[:, h0:h0 + 1], e_tot[:, h1:h1 + 1])
            h_scr[pi] = keep * hp + st


def _ssd_scan_kernel(xaf_ref, xab_ref, dtf_ref, dtb_ref, bias_ref, alog_ref, yf_ref, yb_ref, h_scr):
    @pl.when(pl.program_id(0) == 0)
    def _():
        h_scr[...] = jnp.zeros_like(h_scr)

    _ssd_chunk(xaf_ref, dtf_ref, bias_ref[0], alog_ref[0], yf_ref, h_scr.at[0], False)
    _ssd_chunk(xab_ref, dtb_ref, bias_ref[1], alog_ref[1], yb_ref, h_scr.at[1], True)


def _ssd_scan(xa, p, dt_bias, a_log, *, n_lat):
    n = xa.shape[0]
    nc = n // SSD_CHUNK
    lat_c = n_lat // SSD_CHUNK
    ctx_c = nc - lat_c
    dt_cb = COL_DT // LANES

    def fwd(s):
        return jnp.where(s < ctx_c, lat_c + s, s - ctx_c)

    def bwd(s):
        return nc - 1 - s

    pad = LANES - SSD_HEADS
    bias = jnp.pad(dt_bias, ((0, 0), (0, pad))).reshape(2, 1, LANES)
    alog = jnp.pad(a_log, ((0, 0), (0, pad))).reshape(2, 1, LANES)
    y_sds = jax.ShapeDtypeStruct((n, SSD_WIDTH), F32)
    return pl.pallas_call(
        _ssd_scan_kernel,
        grid=(nc,),
        in_specs=[
            pl.BlockSpec((SSD_CHUNK, SSD_CONV_CH), lambda s: (fwd(s), 0)),
            pl.BlockSpec((SSD_CHUNK, SSD_CONV_CH), lambda s: (bwd(s), 0)),
            pl.BlockSpec((SSD_CHUNK, LANES), lambda s: (fwd(s), dt_cb)),
            pl.BlockSpec((SSD_CHUNK, LANES), lambda s: (bwd(s), dt_cb + 1)),
            pl.BlockSpec((2, 1, LANES), lambda s: (0, 0, 0)),
            pl.BlockSpec((2, 1, LANES), lambda s: (0, 0, 0)),
        ],
        out_specs=[
            pl.BlockSpec((SSD_CHUNK, SSD_WIDTH), lambda s: (fwd(s), 0)),
            pl.BlockSpec((SSD_CHUNK, SSD_WIDTH), lambda s: (bwd(s), 0)),
        ],
        out_shape=[y_sds, y_sds],
        scratch_shapes=[pltpu.VMEM((2, SSD_HEADS // 2, 2 * SSD_HEAD_DIM, SSD_STATE), F32)],
        compiler_params=_cparams(("arbitrary",)),
        name="ssd_scan",
    )(xa, xa, p, p, bias, alog)


def _ssd_out_tile(yf_ref, yb_ref, xs_ref, z_ref, d_ref, g_ref):
    y = yf_ref[...] + yb_ref[...] + d_ref[...] * xs_ref[...]
    y = y * _silu(z_ref[...])
    return _rms(y, g_ref[...]).astype(BF16)


def _mixer_out_kernel(x_ref, attn_ref, yf_ref, yb_ref, xs_ref, z_ref, ap_ref, gp_ref, a_ref, g_ref, an_ref, gn_ref,
                      mod_ref, w_ref, d_ref, sg_ref, cw_ref, cb_ref, lg_ref, lb_ref, o_ref, ext_scr, sh_scr,
                      acc_scr, *, n_lat, lat_tiles):
    i = pl.program_id(0)
    t = x_ref.shape[0]
    first, last = _segment_edges(i, lat_tiles)
    acc_scr[...] = jnp.dot(attn_ref[...], w_ref[0:ATTN_WIDTH, :], preferred_element_type=F32)
    ssd = _ssd_out_tile(yf_ref, yb_ref, xs_ref, z_ref, d_ref, sg_ref)
    cm = _conv_module_tile(ap_ref, gp_ref, a_ref, g_ref, an_ref, gn_ref, cw_ref, cb_ref, lg_ref, lb_ref,
                           ext_scr, sh_scr, first, last)
    rest = jnp.dot(ssd, w_ref[ATTN_WIDTH:ATTN_WIDTH + SSD_WIDTH, :], preferred_element_type=F32)
    rest += jnp.dot(cm, w_ref[ATTN_WIDTH + SSD_WIDTH:, :], preferred_element_type=F32)
    rows = pl.ds(pl.multiple_of(jnp.minimum(i, 0), SUBLANES), t)
    o_ref[...] = x_ref[...] + _mod_row(mod_ref, 2, i * t, n_lat) * (acc_scr[rows, :] + rest)


def _mixer_out(xs, attn, y_f, y_b, xa, p, mod3, w_out, d_skip, ssd_norm_g, dw_w, dw_b, ln_g, ln_b, *, rows, n_lat):
    d = xs.shape[1]
    n = p.shape[0]
    t = ROW_TILE
    hb = t // CM_HALO
    last_hb = n // CM_HALO - 1
    ca = COL_GLU // CM_WIDTH
    cg = ca + 1

    def prev_map(c):
        return lambda i: (jnp.maximum(i * hb - 1, 0), c)

    def next_map(c):
        return lambda i: (jnp.minimum((i + 1) * hb, last_hb), c)

    def row(width, col=0):
        return pl.BlockSpec((t, width), lambda i: (i, col))

    vec = pl.BlockSpec((1, CM_WIDTH), lambda i: (0, 0))
    return pl.pallas_call(
        functools.partial(_mixer_out_kernel, n_lat=n_lat, lat_tiles=n_lat // t),
        grid=(rows // t,),
        in_specs=[
            row(d), row(ATTN_WIDTH),
            row(SSD_WIDTH), row(SSD_WIDTH), row(SSD_WIDTH), row(SSD_WIDTH, COL_Z // SSD_WIDTH),
            pl.BlockSpec((CM_HALO, CM_WIDTH), prev_map(ca)),
            pl.BlockSpec((CM_HALO, CM_WIDTH), prev_map(cg)),
            row(CM_WIDTH, ca), row(CM_WIDTH, cg),
            pl.BlockSpec((CM_HALO, CM_WIDTH), next_map(ca)),
            pl.BlockSpec((CM_HALO, CM_WIDTH), next_map(cg)),
            pl.BlockSpec((3, SUBLANES, d), lambda i: (0, 0, 0)),
            pl.BlockSpec((D_MIX, d), lambda i: (0, 0)),
            vec, vec,
            pl.BlockSpec((CM_KERNEL, CM_WIDTH), lambda i: (0, 0)),
            vec, vec, vec,
        ],
        out_specs=pl.BlockSpec((t, d), lambda i: (i, 0)),
        out_shape=jax.ShapeDtypeStruct((rows, d), F32),
        scratch_shapes=[pltpu.VMEM((t + 2 * CM_HALO, CM_WIDTH), F32),
                        pltpu.VMEM((SUBLANES, t + 2 * CM_HALO - SUBLANES, CM_WIDTH), F32),
                        pltpu.VMEM((t, d), F32)],
        compiler_params=_cparams(("parallel",)),
        name="mixer_out",
    )(xs, attn, y_f, y_b, xa, p, p, p, p, p, p, p, mod3, w_out,
      jnp.repeat(d_skip, SSD_HEAD_DIM).reshape(1, SSD_WIDTH), ssd_norm_g.reshape(1, SSD_WIDTH),
      dw_w, dw_b.reshape(1, -1), ln_g.reshape(1, -1), ln_b.reshape(1, -1))


def _rope_tables(n_lat, n):
    rows = n_lat // GRID_W
    row = jnp.repeat(jnp.arange(rows), GRID_W).astype(F32)
    col = jnp.tile(jnp.arange(GRID_W), rows).astype(F32)
    inv = ROPE_THETA ** (-jnp.arange(0, ROPE_AXIS_DIM, 2, dtype=F32) / ROPE_AXIS_DIM)
    ar = row[:, None] * inv
    ac = col[:, None] * inv
    cos = jnp.concatenate([jnp.cos(ar), jnp.cos(ac), jnp.cos(ar), jnp.cos(ac)], axis=1)
    sin = jnp.concatenate([-jnp.sin(ar), -jnp.sin(ac), jnp.sin(ar), jnp.sin(ac)], axis=1)
    pad = n - n_lat
    cos = jnp.concatenate([cos, jnp.ones((pad, HEAD_DIM), F32)], axis=0)
    sin = jnp.concatenate([sin, jnp.zeros((pad, HEAD_DIM), F32)], axis=0)
    return cos, sin


def _permute_head_dims(a):
    lead = a.shape[:-1]
    quarter = ROPE_AXIS_DIM // 2
    a = a.reshape(*lead, -1, 2, 2, quarter)
    return jnp.swapaxes(a, -3, -2).reshape(*lead, -1)


def _relayout_w_in(w_in):
    d = w_in.shape[0]
    o_k = ATTN_WIDTH
    o_v = o_k + KV_WIDTH
    o_z = o_v + KV_WIDTH
    o_xbc = o_z + SSD_WIDTH
    o_dt = o_xbc + SSD_CONV_CH
    o_glu = o_dt + 2 * SSD_HEADS
    dt = w_in[:, o_dt:o_glu]
    zpad = jnp.zeros((d, LANES - SSD_HEADS), w_in.dtype)
    tail = jnp.zeros((d, P_COLS - COL_DT - 2 * LANES), w_in.dtype)
    return jnp.concatenate([
        _permute_head_dims(w_in[:, :o_k]), _permute_head_dims(w_in[:, o_k:o_v]), w_in[:, o_v:o_z],
        w_in[:, o_xbc:o_dt], w_in[:, o_z:o_xbc], w_in[:, o_glu:],
        dt[:, :SSD_HEADS], zpad, dt[:, SSD_HEADS:], zpad, tail], axis=1).astype(BF16)


def kernel(x, c, ctx, c_ctx, w_mod, b_mod, norm_g, w_ffn_in, w_ffn_out, w_in, w_out, qk_g,
           ssd_conv_w, ssd_conv_b, ssd_dt_bias, ssd_a_log, ssd_d, ssd_norm_g,
           cm_dw_w, cm_dw_b, cm_ln_g, cm_ln_b, final_g):
    assert x.shape[0] == 1 and ctx.shape[0] == 1
    n_lat, d = x.shape[1], x.shape[2]
    n_ctx = ctx.shape[1]
    n = n_lat + n_ctx
    depth = w_mod.shape[0]
    assert n % MM_TILE == 0 and n_lat % ROW_TILE == 0 and n_ctx % ROW_TILE == 0
    assert ROW_TILE % ROW_CHUNK == 0 and MM_TILE % (2 * ROW_CHUNK) == 0
    kv_tile = max(t for t in range(KEY_CHUNK, KV_TILE + 1, KEY_CHUNK) if n % t == 0)
    lat_tm = 512
    assert n_lat % lat_tm == 0 and n_lat % GRID_W == 0

    xs = jnp.concatenate([x[0], ctx[0]], axis=0)
    cvec = jnp.concatenate([c_ctx[None, :], c, jnp.zeros((SUBLANES - 2, d), F32)], axis=0)
    mods = _modulation(cvec, w_mod, b_mod)
    mods = mods.reshape(depth, SUBLANES, N_MOD, d).transpose(0, 2, 1, 3)
    cos, sin = _rope_tables(n_lat, n)
    w_ffn_in_b = w_ffn_in.astype(BF16)
    w_ffn_out_b = w_ffn_out.astype(BF16)
    w_out_b = w_out.astype(BF16)

    out = None
    for l in range(depth):
        last = l == depth - 1
        m = mods[l]
        xs = _ffn(xs, m[0:3], norm_g[l, 0], w_ffn_in_b, w_ffn_out_b, l, 0,
                  rows=n, tm=MM_TILE, n_lat=n_lat)
        q, k, v, p = _in_proj(xs, m[3:6], norm_g[l, 1], _relayout_w_in(w_in[l]), cos, sin,
                              _permute_head_dims(qk_g[l]), tm=MM_TILE, n_lat=n_lat)
        attn = _attention(q, k, v, q_tiles=n_lat // Q_TILE, q_off=0, kv_tiles=n // kv_tile, kv_off=0,
                          tq=Q_TILE, tk=kv_tile)
        if not last:
            attn = _attention(q, k, v, q_tiles=n_ctx // ROW_TILE, q_off=n_lat // ROW_TILE,
                              kv_tiles=n_ctx // ROW_TILE, kv_off=n_lat // ROW_TILE,
                              tq=ROW_TILE, tk=ROW_TILE, prev=attn)
        xa = _ssd_conv(p, ssd_conv_w[l], ssd_conv_b[l], n_lat=n_lat)
        y_f, y_b = _ssd_scan(xa, p, ssd_dt_bias[l], ssd_a_log[l], n_lat=n_lat)
        xs = _mixer_out(xs, attn, y_f, y_b, xa, p, m[3:6], w_out_b[l], ssd_d[l], ssd_norm_g[l],
                        cm_dw_w[l], cm_dw_b[l], cm_ln_g[l], cm_ln_b[l], rows=n_lat if last else n, n_lat=n_lat)
        if last:
            out = _ffn(xs, m[6:9], norm_g[l, 2], w_ffn_in_b, w_ffn_out_b, l, 1,
                       rows=n_lat, tm=lat_tm, n_lat=n_lat, final_g=final_g)
        else:
            xs = _ffn(xs, m[6:9], norm_g[l, 2], w_ffn_in_b, w_ffn_out_b, l, 1,
                      rows=n, tm=MM_TILE, n_lat=n_lat)
    return out[None]
```

```python
import functools
import math

import jax
import jax.numpy as jnp
from jax import lax
from jax.experimental import pallas as pl
from jax.experimental.pallas import tpu as pltpu

F32 = jnp.float32
BF16 = jnp.bfloat16

D_MODEL = 2048
DEPTH = 2
GRID_W = 64
N_MOD = 9
D_FF = 5632
EPS = 1e-6

HEAD_DIM = 128
ATTN_HEADS = 8
ATTN_KV_HEADS = 2
ATTN_GROUPS = ATTN_HEADS // ATTN_KV_HEADS
ATTN_WIDTH = ATTN_HEADS * HEAD_DIM
KV_WIDTH = ATTN_KV_HEADS * HEAD_DIM
ROPE_THETA = 10000.0
ROPE_AXIS_DIM = HEAD_DIM // 2

SSD_HEADS = 8
SSD_HEAD_DIM = 64
SSD_WIDTH = SSD_HEADS * SSD_HEAD_DIM
SSD_GROUPS = 2
SSD_STATE = 128
SSD_CONV = 7
SSD_CHUNK = 128
SSD_CONV_CH = SSD_WIDTH + 2 * SSD_GROUPS * SSD_STATE

CM_WIDTH = 512
CM_KERNEL = 31
D_MIX = ATTN_WIDTH + SSD_WIDTH + CM_WIDTH

LANES = 128
SUBLANES = 8
VMEM_LIMIT = 56 * 1024 * 1024

QKV_COLS = ATTN_WIDTH + 2 * KV_WIDTH
COL_XBC = 0
COL_Z = COL_XBC + SSD_CONV_CH
COL_GLU = COL_Z + SSD_WIDTH
COL_DT = COL_GLU + 2 * CM_WIDTH
P_COLS = 2 * QKV_COLS

ROW_TILE = 256
MM_TILE = 640
FF_CHUNK = 512
KV_TILE = 3328
Q_TILE = 256
EXP_LANES = 1024
SSD_HALO = 8
CM_HALO = 16
CM_ROWS = 64
BF16_ROWS = 16
VT_ROWS = HEAD_DIM + BF16_ROWS
QK_EXP2_SCALE = (HEAD_DIM ** -0.5) * math.log2(math.e)
KEY_CHUNK = 256
UNIT_KEYS = 512
SCORE_BUFFERS = 3
SCORE_AHEAD = 1


def _cparams(sem):
    return pltpu.CompilerParams(dimension_semantics=sem, vmem_limit_bytes=VMEM_LIMIT)


def _sigmoid(x):
    return 1.0 / (1.0 + jnp.exp(-x))


def _silu(x):
    return x * _sigmoid(x)


def _rms(x, g):
    return x * lax.rsqrt(jnp.mean(x * x, axis=-1, keepdims=True) + EPS) * g


ROW_CHUNK = 64


def _mod_row(mod_ref, k, row0, n_lat):
    cls = (row0 < n_lat).astype(jnp.int32)
    return mod_ref[k, pl.ds(cls, 1), :]


def _norm_mod_store(h_scr, x_ref, mod_ref, g_ref, row0, tm, n_lat, start=0):
    for r in range(start, tm, ROW_CHUNK):
        shift = _mod_row(mod_ref, 0, row0 + r, n_lat)
        gain = g_ref[...] * (1.0 + _mod_row(mod_ref, 1, row0 + r, n_lat))
        x = x_ref[r:r + ROW_CHUNK, :]
        inv = lax.rsqrt(jnp.mean(x * x, axis=-1, keepdims=True) + EPS)
        h_scr[r:r + ROW_CHUNK, :] = ((x * inv) * gain + shift).astype(BF16)


def _mod_kernel(c_ref, w_ref, b_ref, o_ref):
    a = _silu(c_ref[...]).astype(BF16)
    o_ref[0] = jnp.dot(a, w_ref[0].astype(BF16), preferred_element_type=F32) + b_ref[0]


def _modulation(cvec, w_mod, b_mod):
    depth, d, n = w_mod.shape
    tn = 1024
    return pl.pallas_call(
        _mod_kernel,
        grid=(depth, n // tn),
        in_specs=[
            pl.BlockSpec((SUBLANES, d), lambda l, j: (0, 0)),
            pl.BlockSpec((1, d, tn), lambda l, j: (l, 0, j)),
            pl.BlockSpec((1, 1, tn), lambda l, j: (l, 0, j)),
        ],
        out_specs=pl.BlockSpec((1, SUBLANES, tn), lambda l, j: (l, 0, j)),
        out_shape=jax.ShapeDtypeStruct((depth, SUBLANES, n), F32),
        compiler_params=_cparams(("arbitrary", "arbitrary")),
        name="modulation",
    )(cvec, w_mod, b_mod.reshape(depth, 1, n))


def _ffn_kernel(x_ref, mod_ref, g_ref, wg_ref, wu_ref, wo_ref, *rest, n_lat, tm, final):
    if final:
        fg_ref, o_ref, h_scr, acc_scr = rest
    else:
        o_ref, h_scr, acc_scr = rest
    i = pl.program_id(0)
    f = pl.program_id(1)
    last = pl.num_programs(1) - 1
    half = tm // 2

    def swiglu_chunk(rows, first):
        h = h_scr[rows, :]
        gt = jnp.dot(h, wg_ref[...], preferred_element_type=F32)
        up = jnp.dot(h, wu_ref[...], preferred_element_type=F32)
        out = jnp.dot((_silu(gt) * up).astype(BF16), wo_ref[...], preferred_element_type=F32)
        if first:
            acc_scr[rows, :] = out
        else:
            acc_scr[rows, :] += out

    def finish(start, stop):
        for r in range(start, stop, ROW_CHUNK):
            rs = slice(r, r + ROW_CHUNK)
            gate = _mod_row(mod_ref, 2, i * tm + r, n_lat)
            y = x_ref[rs, :] + (0.5 * gate) * acc_scr[rs, :]
            if final:
                y = _rms(y, fg_ref[...])
            o_ref[rs, :] = y

    @pl.when(f == 0)
    def _():
        _norm_mod_store(h_scr, x_ref, mod_ref, g_ref, i * tm, half, n_lat)
        swiglu_chunk(slice(0, half), True)
        _norm_mod_store(h_scr, x_ref, mod_ref, g_ref, i * tm, tm, n_lat, start=half)
        swiglu_chunk(slice(half, tm), True)

    @pl.when(jnp.logical_and(f > 0, f < last))
    def _():
        swiglu_chunk(slice(0, tm), False)

    @pl.when(f == last)
    def _():
        swiglu_chunk(slice(0, half), False)
        finish(0, half)
        swiglu_chunk(slice(half, tm), False)
        finish(half, tm)


def _ffn(xs, mod3, norm_g, w_in, w_out, layer, which, *, rows, tm, n_lat, final_g=None):
    d = xs.shape[1]
    nf = D_FF // FF_CHUNK
    final = final_g is not None
    in_specs = [
        pl.BlockSpec((tm, d), lambda i, f: (i, 0)),
        pl.BlockSpec((3, SUBLANES, d), lambda i, f: (0, 0, 0)),
        pl.BlockSpec((1, d), lambda i, f: (0, 0)),
        pl.BlockSpec((None, None, d, FF_CHUNK), lambda i, f: (layer, which, 0, f)),
        pl.BlockSpec((None, None, d, FF_CHUNK), lambda i, f: (layer, which, 0, nf + f)),
        pl.BlockSpec((None, None, FF_CHUNK, d), lambda i, f: (layer, which, f, 0)),
    ]
    args = [xs, mod3, norm_g.reshape(1, d), w_in, w_in, w_out]
    if final:
        in_specs.append(pl.BlockSpec((1, d), lambda i, f: (0, 0)))
        args.append(final_g.reshape(1, d))
    return pl.pallas_call(
        functools.partial(_ffn_kernel, n_lat=n_lat, tm=tm, final=final),
        grid=(rows // tm, nf),
        in_specs=in_specs,
        out_specs=pl.BlockSpec((tm, d), lambda i, f: (i, 0)),
        out_shape=jax.ShapeDtypeStruct((rows, d), F32),
        scratch_shapes=[pltpu.VMEM((tm, d), BF16), pltpu.VMEM((tm, d), F32)],
        compiler_params=_cparams(("parallel", "arbitrary")),
        name="ffn",
    )(*args)


def _proj_kernel(x_ref, mod_ref, g_ref, w_ref, cos_ref, sin_ref, qkg_ref, qo_ref, ko_ref, vo_ref, po_ref,
                 h_scr, qkv_scr, *, n_lat, tm):
    i = pl.program_id(0)
    j = pl.program_id(1)

    @pl.when(j == 0)
    def _():
        _norm_mod_store(h_scr, x_ref, mod_ref, g_ref, i * tm, tm, n_lat)
        qkv_scr[...] = jnp.dot(h_scr[...], w_ref[...], preferred_element_type=F32)

    @pl.when(j == 1)
    def _():
        po_ref[...] = jnp.dot(h_scr[...], w_ref[...], preferred_element_type=F32)
        cos = cos_ref[...]
        sin = sin_ref[...]

        def norm_rope(xh, g):
            xh = _rms(xh, g)
            return xh * cos + pltpu.roll(xh, HEAD_DIM // 2, 1) * sin

        for h in range(ATTN_HEADS):
            sl = slice(h * HEAD_DIM, (h + 1) * HEAD_DIM)
            qh = norm_rope(qkv_scr[:, sl], qkg_ref[0:1, :]) * QK_EXP2_SCALE
            qo_ref[sl, :] = qh.T.astype(BF16)
        for h in range(ATTN_KV_HEADS):
            sl = slice(ATTN_WIDTH + h * HEAD_DIM, ATTN_WIDTH + (h + 1) * HEAD_DIM)
            ko_ref[:, h * HEAD_DIM:(h + 1) * HEAD_DIM] = norm_rope(qkv_scr[:, sl], qkg_ref[1:2, :]).astype(BF16)
        vt = qkv_scr[:, ATTN_WIDTH + KV_WIDTH:].T.astype(BF16)
        ones = jnp.ones((VT_ROWS - HEAD_DIM, tm), BF16)
        for g in range(ATTN_KV_HEADS):
            vo_ref[g * VT_ROWS:g * VT_ROWS + HEAD_DIM, :] = vt[g * HEAD_DIM:(g + 1) * HEAD_DIM, :]
            vo_ref[g * VT_ROWS + HEAD_DIM:(g + 1) * VT_ROWS, :] = ones

    @pl.when(j > 1)
    def _():
        po_ref[...] = jnp.dot(h_scr[...], w_ref[...], preferred_element_type=F32)


def _in_proj(xs, mod3, norm_g, w_in_r, cos, sin, qk_g, *, tm, n_lat):
    n, d = xs.shape
    tn = QKV_COLS
    steps = (QKV_COLS + P_COLS) // tn
    return pl.pallas_call(
        functools.partial(_proj_kernel, n_lat=n_lat, tm=tm),
        grid=(n // tm, steps),
        in_specs=[
            pl.BlockSpec((tm, d), lambda i, j: (i, 0)),
            pl.BlockSpec((3, SUBLANES, d), lambda i, j: (0, 0, 0)),
            pl.BlockSpec((1, d), lambda i, j: (0, 0)),
            pl.BlockSpec((d, tn), lambda i, j: (0, j)),
            pl.BlockSpec((tm, HEAD_DIM), lambda i, j: (i, 0)),
            pl.BlockSpec((tm, HEAD_DIM), lambda i, j: (i, 0)),
            pl.BlockSpec((2, HEAD_DIM), lambda i, j: (0, 0)),
        ],
        out_specs=[
            pl.BlockSpec((ATTN_WIDTH, tm), lambda i, j: (0, i)),
            pl.BlockSpec((tm, KV_WIDTH), lambda i, j: (i, 0)),
            pl.BlockSpec((ATTN_KV_HEADS * VT_ROWS, tm), lambda i, j: (0, i)),
            pl.BlockSpec((tm, tn), lambda i, j: (i, jnp.maximum(j - 1, 0))),
        ],
        out_shape=[
            jax.ShapeDtypeStruct((ATTN_WIDTH, n), BF16),
            jax.ShapeDtypeStruct((n, KV_WIDTH), BF16),
            jax.ShapeDtypeStruct((ATTN_KV_HEADS * VT_ROWS, n), BF16),
            jax.ShapeDtypeStruct((n, P_COLS), F32),
        ],
        scratch_shapes=[pltpu.VMEM((tm, d), BF16), pltpu.VMEM((tm, tn), F32)],
        compiler_params=_cparams(("parallel", "arbitrary")),
        name="in_proj",
    )(xs, mod3, norm_g.reshape(1, d), w_in_r, cos, sin, qk_g)


def _flash_kernel(*refs, aliased):
    if aliased:
        qt_ref, k_ref, vt_ref, _, o_ref, m_scr, acc_scr, *bufs = refs
    else:
        qt_ref, k_ref, vt_ref, o_ref, m_scr, acc_scr, *bufs = refs
    s_bufs, p_bufs = bufs[:len(bufs) // 2], bufs[len(bufs) // 2:]
    j = pl.program_id(1)
    tq = qt_ref.shape[1]
    tk = k_ref.shape[0]
    w = ATTN_GROUPS * tq

    @pl.when(j == 0)
    def _():
        m_scr[...] = jnp.full_like(m_scr, -jnp.inf)
        acc_scr[...] = jnp.zeros_like(acc_scr)

    qs = [jnp.concatenate(
        [qt_ref[(ATTN_GROUPS * g + h) * HEAD_DIM:(ATTN_GROUPS * g + h + 1) * HEAD_DIM, :]
         for h in range(ATTN_GROUPS)], axis=1) for g in range(ATTN_KV_HEADS)]
    uk = min(UNIT_KEYS, tk)
    units = [(g, k0, min(uk, tk - k0)) for k0 in range(0, tk, uk) for g in range(ATTN_KV_HEADS)]

    def scores(u):
        g, k0, size = units[u]
        st = jnp.dot(k_ref[k0:k0 + size, g * HEAD_DIM:(g + 1) * HEAD_DIM], qs[g],
                     preferred_element_type=F32)
        s_bufs[u % len(s_bufs)][0:size, :] = st
        mx = st[0:SUBLANES]
        for r in range(1, size // SUBLANES):
            mx = jnp.maximum(mx, st[r * SUBLANES:(r + 1) * SUBLANES])
        return mx

    m_run = [m_scr[g] for g in range(ATTN_KV_HEADS)]
    row0 = jnp.minimum(j, 0)
    pending = [scores(u) for u in range(min(SCORE_AHEAD, len(units)))]
    for u, (g, k0, size) in enumerate(units):
        mx = pending.pop(0)
        if u + SCORE_AHEAD < len(units):
            pending.append(scores(u + SCORE_AHEAD))
        m_new = jnp.maximum(m_run[g], jnp.max(mx, axis=0, keepdims=True))
        alpha = jnp.exp2(m_run[g] - m_new)
        s_buf, p_buf = s_bufs[u % len(s_bufs)], p_bufs[u % len(p_bufs)]
        for c0 in range(0, w, EXP_LANES):
            cs = slice(c0, min(c0 + EXP_LANES, w))
            m_rep = jnp.broadcast_to(m_new[:, cs], (BF16_ROWS, cs.stop - cs.start))
            for r in range(size // BF16_ROWS):
                rs = slice(r * BF16_ROWS, (r + 1) * BF16_ROWS)
                rd = pl.ds(pl.multiple_of(row0 + r * BF16_ROWS, BF16_ROWS), BF16_ROWS)
                p_buf[rs, cs] = jnp.exp2(s_buf[rd, cs] - m_rep).astype(BF16)
        pv = jnp.dot(vt_ref[g * VT_ROWS:(g + 1) * VT_ROWS, k0:k0 + size], p_buf[0:size, :],
                     preferred_element_type=F32)
        acc_scr[g] = alpha * acc_scr[g] + pv
        m_run[g] = m_new
    for g in range(ATTN_KV_HEADS):
        m_scr[g] = m_run[g]

    @pl.when(j == pl.num_programs(1) - 1)
    def _():
        for g in range(ATTN_KV_HEADS):
            acc = acc_scr[g]
            ot = acc[0:HEAD_DIM] / acc[HEAD_DIM:HEAD_DIM + 1]
            for h in range(ATTN_GROUPS):
                col = (ATTN_GROUPS * g + h) * HEAD_DIM
                o_ref[:, col:col + HEAD_DIM] = ot[:, h * tq:(h + 1) * tq].T.astype(o_ref.dtype)


def _attention(qt, k, vt, *, q_tiles, q_off, kv_tiles, kv_off, tq, tk, prev=None):
    n = qt.shape[1]
    aliased = prev is not None
    in_specs = [
        pl.BlockSpec((ATTN_WIDTH, tq), lambda i, j: (0, q_off + i)),
        pl.BlockSpec((tk, KV_WIDTH), lambda i, j: (kv_off + j, 0)),
        pl.BlockSpec((ATTN_KV_HEADS * VT_ROWS, tk), lambda i, j: (0, kv_off + j)),
    ]
    args = [qt, k, vt]
    if aliased:
        in_specs.append(pl.BlockSpec(memory_space=pl.ANY))
        args.append(prev)
    w = ATTN_GROUPS * tq
    kc = min(UNIT_KEYS, tk)
    units = min(ATTN_KV_HEADS * pl.cdiv(tk, kc), SCORE_BUFFERS)
    return pl.pallas_call(
        functools.partial(_flash_kernel, aliased=aliased),
        grid=(q_tiles, kv_tiles),
        in_specs=in_specs,
        out_specs=pl.BlockSpec((tq, ATTN_WIDTH), lambda i, j: (q_off + i, 0)),
        out_shape=jax.ShapeDtypeStruct((n, ATTN_WIDTH), BF16),
        scratch_shapes=[
            pltpu.VMEM((ATTN_KV_HEADS, 1, w), F32),
            pltpu.VMEM((ATTN_KV_HEADS, VT_ROWS, w), F32),
        ] + [pltpu.VMEM((kc, w), F32)] * units + [pltpu.VMEM((kc, w), BF16)] * units,
        input_output_aliases={3: 0} if aliased else {},
        compiler_params=_cparams(("parallel", "arbitrary")),
        name="attention_ctx" if aliased else "attention",
    )(*args)


def _segment_edges(i, lat_tiles):
    first = jnp.logical_or(i == 0, i == lat_tiles)
    last = jnp.logical_or(i == lat_tiles - 1, i == pl.num_programs(0) - 1)
    return first, last


def _ssd_conv_kernel(prev_ref, cur_ref, next_ref, w_ref, b_ref, o_ref, ext_scr, *, lat_tiles):
    i = pl.program_id(0)
    t = cur_ref.shape[0]
    first, last = _segment_edges(i, lat_tiles)
    ext_scr[0:SSD_HALO, :] = jnp.where(first, 0.0, prev_ref[...])
    ext_scr[SSD_HALO:SSD_HALO + t, :] = cur_ref[...]
    ext_scr[SSD_HALO + t:, :] = jnp.where(last, 0.0, next_ref[...])
    pad = SSD_CONV // 2
    acc = jnp.broadcast_to(b_ref[...], cur_ref.shape)
    for k in range(SSD_CONV):
        acc = acc + w_ref[k:k + 1, :] * ext_scr[pl.ds(SSD_HALO - pad + k, t), :]
    o_ref[...] = _silu(acc)


def _ssd_conv(p, conv_w, conv_b, *, n_lat):
    n = p.shape[0]
    t = ROW_TILE
    hb = t // SSD_HALO
    last_hb = n // SSD_HALO - 1
    cb = COL_XBC // SSD_CONV_CH
    return pl.pallas_call(
        functools.partial(_ssd_conv_kernel, lat_tiles=n_lat // t),
        grid=(n // t,),
        in_specs=[
            pl.BlockSpec((SSD_HALO, SSD_CONV_CH), lambda i: (jnp.maximum(i * hb - 1, 0), cb)),
            pl.BlockSpec((t, SSD_CONV_CH), lambda i: (i, cb)),
            pl.BlockSpec((SSD_HALO, SSD_CONV_CH), lambda i: (jnp.minimum((i + 1) * hb, last_hb), cb)),
            pl.BlockSpec((SSD_CONV, SSD_CONV_CH), lambda i: (0, 0)),
            pl.BlockSpec((1, SSD_CONV_CH), lambda i: (0, 0)),
        ],
        out_specs=pl.BlockSpec((t, SSD_CONV_CH), lambda i: (i, 0)),
        out_shape=jax.ShapeDtypeStruct((n, SSD_CONV_CH), F32),
        scratch_shapes=[pltpu.VMEM((t + 2 * SSD_HALO, SSD_CONV_CH), F32)],
        compiler_params=_cparams(("parallel",)),
        name="ssd_conv",
    )(p, p, p, conv_w, conv_b.reshape(1, SSD_CONV_CH))


def _conv_module_tile(ap_ref, gp_ref, a_ref, g_ref, an_ref, gn_ref, w_ref, b_ref, lg_ref, lb_ref, ext_scr,
                      sh_scr, first, last):
    t = a_ref.shape[0]

    def glu(a, gt):
        return a[...] * _sigmoid(gt[...])

    ext_scr[0:CM_HALO, :] = jnp.where(first, 0.0, glu(ap_ref, gp_ref))
    ext_scr[CM_HALO:CM_HALO + t, :] = glu(a_ref, g_ref)
    ext_scr[CM_HALO + t:, :] = jnp.where(last, 0.0, glu(an_ref, gn_ref))
    pad = CM_KERNEL // 2
    first_off = CM_HALO - pad
    span = t + (first_off + CM_KERNEL - 1) // SUBLANES * SUBLANES
    for phase in range(SUBLANES):
        sh_scr[phase] = ext_scr[phase:phase + span, :]
    blocks = []
    for rb in range(0, t, CM_ROWS):
        acc = jnp.broadcast_to(b_ref[...], (CM_ROWS, a_ref.shape[1]))
        for k in range(CM_KERNEL):
            phase = (first_off + k) % SUBLANES
            row = first_off + k - phase + rb
            acc = acc + w_ref[k:k + 1, :] * sh_scr[phase, row:row + CM_ROWS, :]
        mu = jnp.mean(acc, axis=-1, keepdims=True)
        cen = acc - mu
        var = jnp.mean(cen * cen, axis=-1, keepdims=True)
        y = cen * lax.rsqrt(var + EPS) * lg_ref[...] + lb_ref[...]
        blocks.append(_silu(y).astype(BF16))
    return jnp.concatenate(blocks, axis=0)


def _split3_dot(tri, x):
    hi = x.astype(BF16)
    r1 = x - hi.astype(F32)
    mid = r1.astype(BF16)
    lo = (r1 - mid.astype(F32)).astype(BF16)
    return (jnp.dot(tri, hi, preferred_element_type=F32) + jnp.dot(tri, mid, preferred_element_type=F32)
            + jnp.dot(tri, lo, preferred_element_type=F32))


def _ssd_chunk(xa_ref, dt_ref, bias, alog, y_ref, h_scr, backward):
    q = SSD_CHUNK
    x = dt_ref[...] + bias
    dt = jnp.maximum(x, 0.0) + jnp.log1p(jnp.exp(-jnp.abs(x)))
    da = dt * (-jnp.exp(alog))
    ii = lax.broadcasted_iota(jnp.int32, (q, q), 0)
    jj = lax.broadcasted_iota(jnp.int32, (q, q), 1)
    lane = jj
    inc = (jj >= ii) if backward else (jj <= ii)
    tri = jnp.where(inc, 1.0, 0.0).astype(BF16)
    cum = _split3_dot(tri, da)
    tot = jnp.sum(da, axis=0, keepdims=True)
    cum_t = cum.T
    dt_t = dt.T
    e_in = jnp.exp(cum)
    w_out = jnp.exp(tot - cum) * dt
    e_tot = jnp.exp(tot)

    xa = xa_ref[...]
    pair = 2 * SSD_HEAD_DIM
    low = lane < SSD_HEAD_DIM
    for g in range(SSD_GROUPS):
        b = xa[:, SSD_WIDTH + g * SSD_STATE:SSD_WIDTH + (g + 1) * SSD_STATE].astype(BF16)
        cg = xa[:, SSD_WIDTH + (SSD_GROUPS + g) * SSD_STATE:
                SSD_WIDTH + (SSD_GROUPS + g + 1) * SSD_STATE].astype(BF16)
        cb = lax.dot_general(cg, b, (((1,), (1,)), ((), ())), preferred_element_type=F32)
        for m in range(SSD_HEADS // SSD_GROUPS // 2):
            pi = g * (SSD_HEADS // SSD_GROUPS // 2) + m
            h0, h1 = 2 * pi, 2 * pi + 1
            xp = xa[:, pi * pair:(pi + 1) * pair]
            xp_b = xp.astype(BF16)
            ys = []
            for hd in (h0, h1):
                seg = cum[:, hd:hd + 1] - cum_t[hd:hd + 1, :]
                dec = jnp.exp(jnp.where(inc, seg, -jnp.inf))
                sc = (cb * dec * dt_t[hd:hd + 1, :]).astype(BF16)
                ys.append(jnp.dot(sc, xp_b, preferred_element_type=F32))
            y_diag = jnp.where(low, ys[0], ys[1])
            hp = h_scr[pi]
            y_off = lax.dot_general(cg, hp.astype(BF16), (((1,), (1,)), ((), ())),
                                    preferred_element_type=F32)
            y_off = y_off * jnp.where(low, e_in[:, h0:h0 + 1], e_in[:, h1:h1 + 1])
            y_ref[:, pi * pair:(pi + 1) * pair] = y_diag + y_off
            xw = xp * jnp.where(low, w_out[:, h0:h0 + 1], w_out[:, h1:h1 + 1])
            st = jnp.dot(xw.T.astype(BF16), b, preferred_element_type=F32)
            row_low = ii < SSD_HEAD_DIM
            keep = jnp.where(row_low, e_tot[:, h0:h0 + 1], e_tot[:, h1:h1 + 1])
            h_scr[pi] = keep * hp + st


def _ssd_scan_kernel(xaf_ref, xab_ref, dtf_ref, dtb_ref, bias_ref, alog_ref, yf_ref, yb_ref, h_scr):
    @pl.when(pl.program_id(0) == 0)
    def _():
        h_scr[...] = jnp.zeros_like(h_scr)

    _ssd_chunk(xaf_ref, dtf_ref, bias_ref[0], alog_ref[0], yf_ref, h_scr.at[0], False)
    _ssd_chunk(xab_ref, dtb_ref, bias_ref[1], alog_ref[1], yb_ref, h_scr.at[1], True)


def _ssd_scan(xa, p, dt_bias, a_log, *, n_lat):
    n = xa.shape[0]
    nc = n // SSD_CHUNK
    lat_c = n_lat // SSD_CHUNK
    ctx_c = nc - lat_c
    dt_cb = COL_DT // LANES

    def fwd(s):
        return jnp.where(s < ctx_c, lat_c + s, s - ctx_c)

    def bwd(s):
        return nc - 1 - s

    pad = LANES - SSD_HEADS
    bias = jnp.pad(dt_bias, ((0, 0), (0, pad))).reshape(2, 1, LANES)
    alog = jnp.pad(a_log, ((0, 0), (0, pad))).reshape(2, 1, LANES)
    y_sds = jax.ShapeDtypeStruct((n, SSD_WIDTH), F32)
    return pl.pallas_call(
        _ssd_scan_kernel,
        grid=(nc,),
        in_specs=[
            pl.BlockSpec((SSD_CHUNK, SSD_CONV_CH), lambda s: (fwd(s), 0)),
            pl.BlockSpec((SSD_CHUNK, SSD_CONV_CH), lambda s: (bwd(s), 0)),
            pl.BlockSpec((SSD_CHUNK, LANES), lambda s: (fwd(s), dt_cb)),
            pl.BlockSpec((SSD_CHUNK, LANES), lambda s: (bwd(s), dt_cb + 1)),
            pl.BlockSpec((2, 1, LANES), lambda s: (0, 0, 0)),
            pl.BlockSpec((2, 1, LANES), lambda s: (0, 0, 0)),
        ],
        out_specs=[
            pl.BlockSpec((SSD_CHUNK, SSD_WIDTH), lambda s: (fwd(s), 0)),
            pl.BlockSpec((SSD_CHUNK, SSD_WIDTH), lambda s: (bwd(s), 0)),
        ],
        out_shape=[y_sds, y_sds],
        scratch_shapes=[pltpu.VMEM((2, SSD_HEADS // 2, 2 * SSD_HEAD_DIM, SSD_STATE), F32)],
        compiler_params=_cparams(("arbitrary",)),
        name="ssd_scan",
    )(xa, xa, p, p, bias, alog)


def _ssd_out_tile(yf_ref, yb_ref, xs_ref, z_ref, d_ref, g_ref):
    y = yf_ref[...] + yb_ref[...] + d_ref[...] * xs_ref[...]
    y = y * _silu(z_ref[...])
    return _rms(y, g_ref[...]).astype(BF16)


def _mixer_out_kernel(x_ref, attn_ref, yf_ref, yb_ref, xs_ref, z_ref, ap_ref, gp_ref, a_ref, g_ref, an_ref, gn_ref,
                      mod_ref, w_ref, d_ref, sg_ref, cw_ref, cb_ref, lg_ref, lb_ref, o_ref, ext_scr, sh_scr,
                      acc_scr, *, n_lat, lat_tiles):
    i = pl.program_id(0)
    t = x_ref.shape[0]
    first, last = _segment_edges(i, lat_tiles)
    acc_scr[...] = jnp.dot(attn_ref[...], w_ref[0:ATTN_WIDTH, :], preferred_element_type=F32)
    ssd = _ssd_out_tile(yf_ref, yb_ref, xs_ref, z_ref, d_ref, sg_ref)
    cm = _conv_module_tile(ap_ref, gp_ref, a_ref, g_ref, an_ref, gn_ref, cw_ref, cb_ref, lg_ref, lb_ref,
                           ext_scr, sh_scr, first, last)
    rest = jnp.dot(ssd, w_ref[ATTN_WIDTH:ATTN_WIDTH + SSD_WIDTH, :], preferred_element_type=F32)
    rest += jnp.dot(cm, w_ref[ATTN_WIDTH + SSD_WIDTH:, :], preferred_element_type=F32)
    rows = pl.ds(pl.multiple_of(jnp.minimum(i, 0), SUBLANES), t)
    o_ref[...] = x_ref[...] + _mod_row(mod_ref, 2, i * t, n_lat) * (acc_scr[rows, :] + rest)


def _mixer_out(xs, attn, y_f, y_b, xa, p, mod3, w_out, d_skip, ssd_norm_g, dw_w, dw_b, ln_g, ln_b, *, rows, n_lat):
    d = xs.shape[1]
    n = p.shape[0]
    t = ROW_TILE
    hb = t // CM_HALO
    last_hb = n // CM_HALO - 1
    ca = COL_GLU // CM_WIDTH
    cg = ca + 1

    def prev_map(c):
        return lambda i: (jnp.maximum(i * hb - 1, 0), c)

    def next_map(c):
        return lambda i: (jnp.minimum((i + 1) * hb, last_hb), c)

    def row(width, col=0):
        return pl.BlockSpec((t, width), lambda i: (i, col))

    vec = pl.BlockSpec((1, CM_WIDTH), lambda i: (0, 0))
    return pl.pallas_call(
        functools.partial(_mixer_out_kernel, n_lat=n_lat, lat_tiles=n_lat // t),
        grid=(rows // t,),
        in_specs=[
            row(d), row(ATTN_WIDTH),
            row(SSD_WIDTH), row(SSD_WIDTH), row(SSD_WIDTH), row(SSD_WIDTH, COL_Z // SSD_WIDTH),
            pl.BlockSpec((CM_HALO, CM_WIDTH), prev_map(ca)),
            pl.BlockSpec((CM_HALO, CM_WIDTH), prev_map(cg)),
            row(CM_WIDTH, ca), row(CM_WIDTH, cg),
            pl.BlockSpec((CM_HALO, CM_WIDTH), next_map(ca)),
            pl.BlockSpec((CM_HALO, CM_WIDTH), next_map(cg)),
            pl.BlockSpec((3, SUBLANES, d), lambda i: (0, 0, 0)),
            pl.BlockSpec((D_MIX, d), lambda i: (0, 0)),
            vec, vec,
            pl.BlockSpec((CM_KERNEL, CM_WIDTH), lambda i: (0, 0)),
            vec, vec, vec,
        ],
        out_specs=pl.BlockSpec((t, d), lambda i: (i, 0)),
        out_shape=jax.ShapeDtypeStruct((rows, d), F32),
        scratch_shapes=[pltpu.VMEM((t + 2 * CM_HALO, CM_WIDTH), F32),
                        pltpu.VMEM((SUBLANES, t + 2 * CM_HALO - SUBLANES, CM_WIDTH), F32),
                        pltpu.VMEM((t, d), F32)],
        compiler_params=_cparams(("parallel",)),
        name="mixer_out",
    )(xs, attn, y_f, y_b, xa, p, p, p, p, p, p, p, mod3, w_out,
      jnp.repeat(d_skip, SSD_HEAD_DIM).reshape(1, SSD_WIDTH), ssd_norm_g.reshape(1, SSD_WIDTH),
      dw_w, dw_b.reshape(1, -1), ln_g.reshape(1, -1), ln_b.reshape(1, -1))


def _rope_tables(n_lat, n):
    rows = n_lat // GRID_W
    quarter = ROPE_AXIS_DIM // 2
    inv = ROPE_THETA ** (-jnp.arange(0, ROPE_AXIS_DIM, 2, dtype=F32) / ROPE_AXIS_DIM)
    ar = jnp.arange(rows).astype(F32)[:, None] * inv
    ac = jnp.arange(GRID_W).astype(F32)[:, None] * inv

    def per_token(by_row, by_col):
        r = jnp.broadcast_to(by_row[:, None, :], (rows, GRID_W, quarter)).reshape(n_lat, quarter)
        c = jnp.broadcast_to(by_col[None, :, :], (rows, GRID_W, quarter)).reshape(n_lat, quarter)
        return r, c

    cr, cc = per_token(jnp.cos(ar), jnp.cos(ac))
    sr, sc = per_token(jnp.sin(ar), jnp.sin(ac))
    cos = jnp.concatenate([cr, cc, cr, cc], axis=1)
    sin = jnp.concatenate([-sr, -sc, sr, sc], axis=1)
    pad = n - n_lat
    cos = jnp.concatenate([cos, jnp.ones((pad, HEAD_DIM), F32)], axis=0)
    sin = jnp.concatenate([sin, jnp.zeros((pad, HEAD_DIM), F32)], axis=0)
    return cos, sin


def _permute_head_dims(a):
    lead = a.shape[:-1]
    quarter = ROPE_AXIS_DIM // 2
    a = a.reshape(*lead, -1, 2, 2, quarter)
    return jnp.swapaxes(a, -3, -2).reshape(*lead, -1)


def _relayout_w_in(w_in):
    d = w_in.shape[0]
    o_k = ATTN_WIDTH
    o_v = o_k + KV_WIDTH
    o_z = o_v + KV_WIDTH
    o_xbc = o_z + SSD_WIDTH
    o_dt = o_xbc + SSD_CONV_CH
    o_glu = o_dt + 2 * SSD_HEADS
    dt = w_in[:, o_dt:o_glu]
    zpad = jnp.zeros((d, LANES - SSD_HEADS), w_in.dtype)
    tail = jnp.zeros((d, P_COLS - COL_DT - 2 * LANES), w_in.dtype)
    return jnp.concatenate([
        _permute_head_dims(w_in[:, :o_k]), _permute_head_dims(w_in[:, o_k:o_v]), w_in[:, o_v:o_z],
        w_in[:, o_xbc:o_dt], w_in[:, o_z:o_xbc], w_in[:, o_glu:],
        dt[:, :SSD_HEADS], zpad, dt[:, SSD_HEADS:], zpad, tail], axis=1).astype(BF16)


def kernel(x, c, ctx, c_ctx, w_mod, b_mod, norm_g, w_ffn_in, w_ffn_out, w_in, w_out, qk_g,
           ssd_conv_w, ssd_conv_b, ssd_dt_bias, ssd_a_log, ssd_d, ssd_norm_g,
           cm_dw_w, cm_dw_b, cm_ln_g, cm_ln_b, final_g):
    assert x.shape[0] == 1 and ctx.shape[0] == 1
    n_lat, d = x.shape[1], x.shape[2]
    n_ctx = ctx.shape[1]
    n = n_lat + n_ctx
    depth = w_mod.shape[0]
    assert n % MM_TILE == 0 and n_lat % ROW_TILE == 0 and n_ctx % ROW_TILE == 0
    assert ROW_TILE % ROW_CHUNK == 0 and MM_TILE % (2 * ROW_CHUNK) == 0
    kv_tile = max(t for t in range(KEY_CHUNK, KV_TILE + 1, KEY_CHUNK) if n % t == 0)
    lat_tm = 512
    assert n_lat % lat_tm == 0 and n_lat % GRID_W == 0

    xs = jnp.concatenate([x[0], ctx[0]], axis=0)
    cvec = jnp.concatenate([c_ctx[None, :], c, jnp.zeros((SUBLANES - 2, d), F32)], axis=0)
    mods = _modulation(cvec, w_mod, b_mod)
    mods = mods.reshape(depth, SUBLANES, N_MOD, d).transpose(0, 2, 1, 3)
    cos, sin = _rope_tables(n_lat, n)
    w_ffn_in_b = w_ffn_in.astype(BF16)
    w_ffn_out_b = w_ffn_out.astype(BF16)
    w_out_b = w_out.astype(BF16)

    out = None
    for l in range(depth):
        last = l == depth - 1
        m = mods[l]
        xs = _ffn(xs, m[0:3], norm_g[l, 0], w_ffn_in_b, w_ffn_out_b, l, 0,
                  rows=n, tm=MM_TILE, n_lat=n_lat)
        q, k, v, p = _in_proj(xs, m[3:6], norm_g[l, 1], _relayout_w_in(w_in[l]), cos, sin,
                              _permute_head_dims(qk_g[l]), tm=MM_TILE, n_lat=n_lat)
        attn = _attention(q, k, v, q_tiles=n_lat // Q_TILE, q_off=0, kv_tiles=n // kv_tile, kv_off=0,
                          tq=Q_TILE, tk=kv_tile)
        if not last:
            attn = _attention(q, k, v, q_tiles=n_ctx // ROW_TILE, q_off=n_lat // ROW_TILE,
                              kv_tiles=n_ctx // ROW_TILE, kv_off=n_lat // ROW_TILE,
                              tq=ROW_TILE, tk=ROW_TILE, prev=attn)
        xa = _ssd_conv(p, ssd_conv_w[l], ssd_conv_b[l], n_lat=n_lat)
        y_f, y_b = _ssd_scan(xa, p, ssd_dt_bias[l], ssd_a_log[l], n_lat=n_lat)
        xs = _mixer_out(xs, attn, y_f, y_b, xa, p, m[3:6], w_out_b[l], ssd_d[l], ssd_norm_g[l],
                        cm_dw_w[l], cm_dw_b[l], cm_ln_g[l], cm_ln_b[l], rows=n_lat if last else n, n_lat=n_lat)
        if last:
            out = _ffn(xs, m[6:9], norm_g[l, 2], w_ffn_in_b, w_ffn_out_b, l, 1,
                       rows=n_lat, tm=lat_tm, n_lat=n_lat, final_g=final_g)
        else:
            xs = _ffn(xs, m[6:9], norm_g[l, 2], w_ffn_in_b, w_ffn_out_b, l, 1,
                      rows=n, tm=MM_TILE, n_lat=n_lat)
    return out[None]
```
